```python
import math
import numpy as np
import jax
import jax.numpy as jnp
from jax import lax

D_MODEL = 2048
BATCH = 2
SEQ = 16384
DEPTH = 1
DEC_BATCH = 8
DEC_SEQ = 32
PAST_LEN = 4096

CHUNK = 64
Q_BLOCK = 128
EPS = 1e-6

N_HEADS = 16
HEAD_DIM = 64
N_KV_HEADS = 4
GROUP = N_HEADS // N_KV_HEADS
IDX_HEADS = 16
IDX_DIM = 64
TOPK_MAX = 256
N_BUCKETS = 32
MAX_DISTANCE = 128

REC_HEADS = 8
REC_DK = 128
REC_DV = 128

D_FF = 5632

ATTN_WIDTH = N_HEADS * HEAD_DIM
KV_WIDTH = N_KV_HEADS * HEAD_DIM
REC_K_WIDTH = REC_HEADS * REC_DK
REC_V_WIDTH = REC_HEADS * REC_DV
IN_WIDTHS = (ATTN_WIDTH, KV_WIDTH, KV_WIDTH, IDX_HEADS * IDX_DIM, IDX_DIM, IDX_HEADS,
             REC_K_WIDTH, REC_K_WIDTH, REC_V_WIDTH, REC_V_WIDTH, D_MODEL, D_MODEL)
IN_SPLITS = tuple(int(s) for s in np.cumsum(IN_WIDTHS)[:-1])
IN_TOTAL = int(sum(IN_WIDTHS))

kernel_name = 'hybrid_stream_dsa_hgrn2_macaron'


def rmsnorm(x, g):
    xf = x.astype(jnp.float32)
    y = xf * lax.rsqrt(jnp.mean(xf * xf, axis=-1, keepdims=True) + EPS)
    return (y * g.astype(jnp.float32)).astype(x.dtype)


def swiglu_half(x, norm_g, w_in, w_out):
    h = rmsnorm(x, norm_g)
    gate, up = jnp.split(h @ w_in, 2, axis=-1)
    return 0.5 * ((jax.nn.silu(gate) * up) @ w_out)


def t5_bucket(rel):
    half = N_BUCKETS // 2
    max_exact = half // 2
    n = jnp.abs(rel)
    large = max_exact + (jnp.log(jnp.maximum(n, max_exact).astype(jnp.float32) / max_exact)
                         / math.log(MAX_DISTANCE / max_exact) * (half - max_exact)).astype(jnp.int32)
    large = jnp.minimum(large, half - 1)
    return jnp.where(rel > 0, half, 0) + jnp.where(n < max_exact, n, large)


def sparse_attend(q, qi, wi, q_pos, k, v, ki, rel_bias, topk):
    B, Tq = q.shape[0], q.shape[1]
    L = k.shape[1]
    limit = (q_pos // CHUNK + 1) * CHUNK
    visible = jnp.arange(L, dtype=jnp.int32)[None, :] < limit[:, None]
    dots = jnp.einsum('bqhd,bsd->bqhs', qi.astype(jnp.float32), ki.astype(jnp.float32))
    score = jnp.einsum('bqhs,bqh->bqs', jax.nn.relu(dots), wi.astype(jnp.float32))
    score = jnp.where(visible[None], score, -jnp.inf)
    top_val, idx = lax.top_k(score, topk)
    valid = top_val > -jnp.inf
    gather = jax.vmap(lambda rows, sel: rows[sel])
    k_sel = gather(k, idx)
    v_sel = gather(v, idx)
    bias = rel_bias[t5_bucket(idx - q_pos[None, :, None])].astype(jnp.float32)
    bias = bias.reshape(B, Tq, topk, N_KV_HEADS, GROUP).transpose(0, 1, 3, 4, 2)
    qg = q.reshape(B, Tq, N_KV_HEADS, GROUP, HEAD_DIM)
    logits = jnp.einsum('bqngd,bqknd->bqngk', qg, k_sel).astype(jnp.float32) * (HEAD_DIM ** -0.5) + bias
    logits = jnp.where(valid[:, :, None, None, :], logits, -jnp.inf)
    p = jax.nn.softmax(logits, axis=-1).astype(v.dtype)
    o = jnp.einsum('bqngk,bqknd->bqngd', p, v_sel)
    return o.reshape(B, Tq, ATTN_WIDTH)


def sparse_attention(q, qi, wi, q_pos, k, v, ki, rel_bias, topk):
    B, T = q.shape[0], q.shape[1]
    blk = min(Q_BLOCK, T)
    nb = T // blk

    def split(a):
        return a.reshape((B, nb, blk) + a.shape[2:]).swapaxes(0, 1)

    def one(args):
        qb, qib, wib, pb = args
        return sparse_attend(qb, qib, wib, pb, k, v, ki, rel_bias, topk)

    o = lax.map(one, (split(q), split(qi), split(wi), q_pos.reshape(nb, blk)))
    return o.swapaxes(0, 1).reshape(B, T, ATTN_WIDTH)


def hgrn2_recurrence(q, k, v, logf, s0):
    B, T = q.shape[0], q.shape[1]
    C = min(CHUNK, T)
    n = T // C

    def chunks(a):
        return a.reshape(B, n, C, REC_HEADS, a.shape[-1]).transpose(1, 0, 3, 2, 4)

    causal = jnp.tril(jnp.ones((C, C), dtype=bool))[:, :, None]

    def step(S, inp):
        qc, kc, vc, gc = inp
        b = jnp.cumsum(gc, axis=2)
        diff = b[:, :, :, None, :] - b[:, :, None, :, :]
        decay = jnp.exp(jnp.where(causal, diff, -jnp.inf))
        scores = jnp.einsum('bhtk,bhsk,bhtsk->bhts', qc, kc, decay)
        o = (jnp.einsum('bhtk,bhkv->bhtv', qc * jnp.exp(b), S)
             + jnp.einsum('bhts,bhsv->bhtv', scores, vc))
        b_end = b[:, :, -1:, :]
        S = (jnp.exp(b_end[:, :, 0, :])[..., None] * S
             + jnp.einsum('bhsk,bhsv->bhkv', kc * jnp.exp(b_end - b), vc))
        return S, o

    S, o = lax.scan(step, s0, (chunks(q), chunks(k), chunks(v), chunks(logf)))
    return o.transpose(1, 0, 3, 2, 4).reshape(B, T, REC_HEADS, REC_DV), S


def token_mixer(h, past_k, past_v, past_ki, s0, lb, w_in, g_q, g_k, rel_bias, g_rec,
                w_attn, w_rec, w_out, topk):
    B, T = h.shape[0], h.shape[1]
    P = past_k.shape[1]
    q, k, v, qi, ki, wi, rf, rq, ri, rg, ga, gb = jnp.split(h @ w_in, IN_SPLITS, axis=-1)
    q = rmsnorm(q.reshape(B, T, N_HEADS, HEAD_DIM), g_q)
    k = rmsnorm(k.reshape(B, T, N_KV_HEADS, HEAD_DIM), g_k)
    v = v.reshape(B, T, N_KV_HEADS, HEAD_DIM)
    qi = qi.reshape(B, T, IDX_HEADS, IDX_DIM)
    wi = wi * (IDX_HEADS ** -0.5)
    q_pos = P + jnp.arange(T, dtype=jnp.int32)
    k_all = jnp.concatenate([past_k.astype(k.dtype), k], axis=1)
    v_all = jnp.concatenate([past_v.astype(v.dtype), v], axis=1)
    ki_all = jnp.concatenate([past_ki.astype(ki.dtype), ki], axis=1)
    o_attn = sparse_attention(q, qi, wi, q_pos, k_all, v_all, ki_all, rel_bias, topk)
    fgate = lb + (1.0 - lb) * jax.nn.sigmoid(rf.astype(jnp.float32))
    shp_k = (B, T, REC_HEADS, REC_DK)
    logf = jnp.log(fgate).reshape(shp_k)
    kk = (1.0 - fgate).reshape(shp_k)
    qq = jax.nn.silu(rq.astype(jnp.float32)).reshape(shp_k)
    vv = ri.astype(jnp.float32).reshape(B, T, REC_HEADS, REC_DV)
    o_rec, S = hgrn2_recurrence(qq, kk, vv, logf, s0.astype(jnp.float32))
    o_rec = rmsnorm(o_rec, g_rec).reshape(B, T, REC_V_WIDTH) * jax.nn.silu(rg.astype(jnp.float32))
    o_rec = o_rec.astype(h.dtype)
    merged = jax.nn.sigmoid(ga) * (o_attn @ w_attn) + jax.nn.sigmoid(gb) * (o_rec @ w_rec)
    return merged @ w_out, k, v, ki, S.astype(s0.dtype)


def trunk(x, past_k, past_v, past_ki, s0, topk, lower_bounds, norm_ffa, w_ffa_in, w_ffa_out,
          norm_mix, w_in, qk_gain_q, qk_gain_k, rel_bias, hgrn_norm, w_branch_attn,
          w_branch_rec, w_out, norm_ffb, w_ffb_in, w_ffb_out):
    ks, vs, kis, ss = [], [], [], []
    for l in range(DEPTH):
        x = x + swiglu_half(x, norm_ffa[l], w_ffa_in[l], w_ffa_out[l])
        mix, k, v, ki, S = token_mixer(rmsnorm(x, norm_mix[l]), past_k[l], past_v[l], past_ki[l], s0[l],
                                       lower_bounds[l], w_in[l], qk_gain_q[l], qk_gain_k[l], rel_bias,
                                       hgrn_norm[l], w_branch_attn[l], w_branch_rec[l], w_out[l], topk)
        x = x + mix
        x = x + swiglu_half(x, norm_ffb[l], w_ffb_in[l], w_ffb_out[l])
        ks.append(k)
        vs.append(v)
        kis.append(ki)
        ss.append(S)
    return x, jnp.stack(ks), jnp.stack(vs), jnp.stack(kis), jnp.stack(ss)


def setup_inputs(seed: int = 0) -> dict:
    key = jax.random.key(seed)
    kk = jax.random.split(key, 24)

    def nrm(k, shape, scale=1.0):
        return jax.random.normal(k, shape, jnp.float32) * scale

    def gain(k, shape):
        return 1.0 + 0.05 * jax.random.normal(k, shape, jnp.float32)

    return {
        'x_prompt': nrm(kk[0], (BATCH, SEQ, D_MODEL)),
        'x_sample': nrm(kk[1], (DEC_BATCH, DEC_SEQ, D_MODEL)),
        'cache_k': nrm(kk[2], (DEPTH, DEC_BATCH, PAST_LEN, N_KV_HEADS, HEAD_DIM)),
        'cache_v': nrm(kk[3], (DEPTH, DEC_BATCH, PAST_LEN, N_KV_HEADS, HEAD_DIM)),
        'cache_kidx': nrm(kk[4], (DEPTH, DEC_BATCH, PAST_LEN, IDX_DIM)),
        'state_hgrn': nrm(kk[5], (DEPTH, DEC_BATCH, REC_HEADS, REC_DK, REC_DV), 0.3),
        'norm_ffa': gain(kk[6], (DEPTH, D_MODEL)),
        'w_ffa_in': nrm(kk[7], (DEPTH, D_MODEL, 2 * D_FF), D_MODEL ** -0.5),
        'w_ffa_out': nrm(kk[8], (DEPTH, D_FF, D_MODEL), D_FF ** -0.5),
        'norm_mix': gain(kk[9], (DEPTH, D_MODEL)),
        'w_in': nrm(kk[10], (DEPTH, D_MODEL, IN_TOTAL), D_MODEL ** -0.5),
        'qk_gain_q': gain(kk[11], (DEPTH, HEAD_DIM)),
        'qk_gain_k': gain(kk[12], (DEPTH, HEAD_DIM)),
        'rel_bias': nrm(kk[13], (N_BUCKETS, N_HEADS), 0.5),
        'hgrn_lb_raw': nrm(kk[14], (DEPTH + 1, REC_K_WIDTH)),
        'hgrn_norm': gain(kk[15], (DEPTH, REC_DV)),
        'w_branch_attn': nrm(kk[16], (DEPTH, ATTN_WIDTH, D_MODEL), ATTN_WIDTH ** -0.5),
        'w_branch_rec': nrm(kk[17], (DEPTH, REC_V_WIDTH, D_MODEL), REC_V_WIDTH ** -0.5),
        'w_out': nrm(kk[18], (DEPTH, D_MODEL, D_MODEL), D_MODEL ** -0.5),
        'norm_ffb': gain(kk[19], (DEPTH, D_MODEL)),
        'w_ffb_in': nrm(kk[20], (DEPTH, D_MODEL, 2 * D_FF), D_MODEL ** -0.5),
        'w_ffb_out': nrm(kk[21], (DEPTH, D_FF, D_MODEL), D_FF ** -0.5),
    }


def reference(x_prompt, x_sample, cache_k, cache_v, cache_kidx, state_hgrn, norm_ffa, w_ffa_in,
              w_ffa_out, norm_mix, w_in, qk_gain_q, qk_gain_k, rel_bias, hgrn_lb_raw, hgrn_norm,
              w_branch_attn, w_branch_rec, w_out, norm_ffb, w_ffb_in, w_ffb_out):
    lb_p = jax.nn.softmax(hgrn_lb_raw.astype(jnp.float32), axis=0)
    lower_bounds = jnp.cumsum(lb_p, axis=0)[:DEPTH]
    Bp, Tp = x_prompt.shape[0], x_prompt.shape[1]
    topk_prompt = min(TOPK_MAX, Tp // 4)
    topk_sample = min(TOPK_MAX, (cache_k.shape[2] + x_sample.shape[1]) // 4)
    dt = x_prompt.dtype
    empty_k = jnp.zeros((DEPTH, Bp, 0, N_KV_HEADS, HEAD_DIM), dt)
    empty_ki = jnp.zeros((DEPTH, Bp, 0, IDX_DIM), dt)
    zero_s = jnp.zeros((DEPTH, Bp, REC_HEADS, REC_DK, REC_DV), state_hgrn.dtype)
    weights = (norm_ffa, w_ffa_in, w_ffa_out, norm_mix, w_in, qk_gain_q, qk_gain_k, rel_bias, hgrn_norm,
               w_branch_attn, w_branch_rec, w_out, norm_ffb, w_ffb_in, w_ffb_out)
    y_prompt, k_prompt, v_prompt, kidx_prompt, hgrn_prompt = trunk(
        x_prompt, empty_k, empty_k, empty_ki, zero_s, topk_prompt, lower_bounds, *weights)
    y_sample, k_sample, v_sample, kidx_sample, hgrn_sample = trunk(
        x_sample, cache_k, cache_v, cache_kidx, state_hgrn, topk_sample, lower_bounds, *weights)
    return (y_prompt, y_sample, k_prompt, v_prompt, kidx_prompt, hgrn_prompt,
            k_sample, v_sample, kidx_sample, hgrn_sample)
```

```python
import functools
import math

import numpy as np
import jax
import jax.numpy as jnp
from jax import lax
from jax.experimental import pallas as pl
from jax.experimental.pallas import tpu as pltpu

F32 = jnp.float32
BF16 = jnp.bfloat16
I32 = jnp.int32

EPS = 1e-6
CHUNK = 64
HEAD_DIM = 64
N_HEADS = 16
N_KV_HEADS = 4
GROUP = N_HEADS // N_KV_HEADS
IDX_HEADS = 16
IDX_DIM = 64
TOPK_MAX = 256
N_BUCKETS = 32
MAX_DISTANCE = 128
REC_HEADS = 8
REC_DK = 128
REC_DV = 128
ATTN_WIDTH = N_HEADS * HEAD_DIM
KV_WIDTH = N_KV_HEADS * HEAD_DIM
REC_WIDTH = REC_HEADS * REC_DK

LANES = 128
KEY_TILE = LANES
FAR_CHUNK = 2 * KEY_TILE
COUNT_TILES = 4
SUB = 16
VMEM_LIMIT_BYTES = 56 * 1024 * 1024

INT_MIN = -2 ** 31
KEY_NEG_INF = int(np.array(-np.inf, np.float32).view(np.int32)) ^ 0x7FFFFFFF
NEG_BIG = -1e30


def _params(sem):
    return pltpu.CompilerParams(dimension_semantics=sem, vmem_limit_bytes=VMEM_LIMIT_BYTES)


def _dot(a, b):
    return jnp.dot(a, b, preferred_element_type=F32)


def _dot_nt(a, b):
    return lax.dot_general(a, b, (((1,), (1,)), ((), ())), preferred_element_type=F32)


def _split3(x):
    x1 = x.astype(BF16)
    r1 = x - x1.astype(F32)
    x2 = r1.astype(BF16)
    x3 = (r1 - x2.astype(F32)).astype(BF16)
    return x1, x2, x3


def _dot_sel_rhs(x, sel):
    x1, x2, x3 = _split3(x)
    return _dot(x1, sel) + _dot(x2, sel) + _dot(x3, sel)


def _dot_sel_lhs(sel, x):
    x1, x2, x3 = _split3(x)
    return _dot(sel, x1) + _dot(sel, x2) + _dot(sel, x3)


def _rms_rows(x, g):
    ms = jnp.mean(x * x, axis=-1, keepdims=True)
    return x * lax.rsqrt(ms + EPS) * g


def _silu(x):
    return x * jax.nn.sigmoid(x)


def _ffn_kernel(x_ref, g_ref, wg_ref, wu_ref, wo_ref, o_ref, h_ref):
    @pl.when(pl.program_id(1) == 0)
    def _():
        x = x_ref[...]
        h_ref[...] = _rms_rows(x, g_ref[...]).astype(BF16)
        o_ref[...] = x

    h = h_ref[...]
    gate = _dot(h, wg_ref[...])
    up = _dot(h, wu_ref[...])
    a = (_silu(gate) * up * 0.5).astype(BF16)
    o_ref[...] += _dot(a, wo_ref[...])


def _ffn(x, g, w_in, w_out, tm, tf):
    n, d = x.shape
    dff = w_out.shape[0]
    nf = dff // tf
    return pl.pallas_call(
        _ffn_kernel,
        grid=(n // tm, nf),
        in_specs=[
            pl.BlockSpec((tm, d), lambda i, j: (i, 0)),
            pl.BlockSpec((1, d), lambda i, j: (0, 0)),
            pl.BlockSpec((d, tf), lambda i, j: (0, j)),
            pl.BlockSpec((d, tf), lambda i, j: (0, j + nf)),
            pl.BlockSpec((tf, d), lambda i, j: (j, 0)),
        ],
        out_specs=pl.BlockSpec((tm, d), lambda i, j: (i, 0)),
        out_shape=jax.ShapeDtypeStruct((n, d), F32),
        scratch_shapes=[pltpu.VMEM((tm, d), BF16)],
        compiler_params=_params(("arbitrary", "arbitrary")),
        name="ffn",
    )(x, g, w_in, w_in, w_out)


def _norm_proj_kernel(x_ref, g_ref, w_ref, o_ref, h_ref):
    @pl.when(pl.program_id(1) == 0)
    def _():
        h_ref[...] = _rms_rows(x_ref[...], g_ref[...]).astype(BF16)

    o_ref[...] = _dot(h_ref[...], w_ref[...]).astype(o_ref.dtype)


def _norm_proj(x, g, w, tm, tn, name):
    n, d = x.shape
    width = w.shape[1]
    return pl.pallas_call(
        _norm_proj_kernel,
        grid=(n // tm, width // tn),
        in_specs=[
            pl.BlockSpec((tm, d), lambda i, j: (i, 0)),
            pl.BlockSpec((1, d), lambda i, j: (0, 0)),
            pl.BlockSpec((d, tn), lambda i, j: (0, j)),
        ],
        out_specs=pl.BlockSpec((tm, tn), lambda i, j: (i, j)),
        out_shape=jax.ShapeDtypeStruct((n, width), F32),
        scratch_shapes=[pltpu.VMEM((tm, d), BF16)],
        compiler_params=_params(("arbitrary", "arbitrary")),
        name=name,
    )(x, g, w)


_C_Q = 0
_C_K = _C_Q + ATTN_WIDTH
_C_V = _C_K + KV_WIDTH
_C_QI = _C_V + KV_WIDTH
_C_KI = _C_QI + IDX_HEADS * IDX_DIM
_C_WI = _C_KI + LANES
_C_END = _C_WI + LANES


def _head_rms(x, gsum_ref, gexp_ref, gain_ref):
    ss = _dot_sel_rhs(x * x, gsum_ref[...])
    r = lax.rsqrt(ss * (1.0 / HEAD_DIM) + EPS)
    return x * _dot_sel_rhs(r, gexp_ref[...]) * gain_ref[...]


def _attn_prep_kernel(x_ref, g_ref, w_ref, gq_ref, gk_ref, sq_ref, eq_ref, sk_ref, ek_ref,
                      q_ref, k_ref, kb_ref, v_ref, vb_ref, qi_ref, ki_ref, kib_ref, wi_ref):
    h = _rms_rows(x_ref[...], g_ref[...]).astype(BF16)
    q = _dot(h, w_ref[:, _C_Q:_C_K])
    q_ref[...] = _head_rms(q, sq_ref, eq_ref, gq_ref).astype(BF16)
    k = _head_rms(_dot(h, w_ref[:, _C_K:_C_V]), sk_ref, ek_ref, gk_ref)
    k_ref[...] = k
    kb_ref[...] = k.astype(BF16)
    v = _dot(h, w_ref[:, _C_V:_C_QI])
    v_ref[...] = v
    vb_ref[...] = v.astype(BF16)
    qi_ref[...] = _dot(h, w_ref[:, _C_QI:_C_KI]).astype(BF16)
    ki = _dot(h, w_ref[:, _C_KI:_C_WI])[:, :IDX_DIM]
    ki_ref[...] = ki
    kib_ref[...] = ki.astype(BF16)
    wi_ref[...] = _dot(h, w_ref[:, _C_WI:_C_END])[:, :IDX_HEADS] * (IDX_HEADS ** -0.5)


def _head_selectors(n_heads):
    width = n_heads * HEAD_DIM
    col_head = np.arange(width) // HEAD_DIM
    gsum = (col_head[:, None] == np.arange(LANES)[None, :]).astype(np.float32)
    return jnp.asarray(gsum, BF16), jnp.asarray(gsum.T, BF16)


def _attn_prep(x, g, w, gq, gk, tm):
    n, d = x.shape
    sq, eq = _head_selectors(N_HEADS)
    sk, ek = _head_selectors(N_KV_HEADS)
    row = lambda width: pl.BlockSpec((tm, width), lambda i: (i, 0))
    full = lambda a: pl.BlockSpec(a.shape, lambda i: (0,) * a.ndim, pipeline_mode=pl.Buffered(1))
    out_shapes = [
        jax.ShapeDtypeStruct((n, ATTN_WIDTH), BF16),
        jax.ShapeDtypeStruct((n, KV_WIDTH), F32),
        jax.ShapeDtypeStruct((n, KV_WIDTH), BF16),
        jax.ShapeDtypeStruct((n, KV_WIDTH), F32),
        jax.ShapeDtypeStruct((n, KV_WIDTH), BF16),
        jax.ShapeDtypeStruct((n, IDX_HEADS * IDX_DIM), BF16),
        jax.ShapeDtypeStruct((n, IDX_DIM), F32),
        jax.ShapeDtypeStruct((n, IDX_DIM), BF16),
        jax.ShapeDtypeStruct((n, IDX_HEADS), F32),
    ]
    return pl.pallas_call(
        _attn_prep_kernel,
        grid=(n // tm,),
        in_specs=[row(d), full(g), full(w), full(gq), full(gk), full(sq), full(eq), full(sk), full(ek)],
        out_specs=[row(s.shape[1]) for s in out_shapes],
        out_shape=out_shapes,
        compiler_params=_params(("arbitrary",)),
        name="attn_prep",
    )(x, g, w, gq, gk, sq, eq, sk, ek)


def _sortable(x):
    bits = lax.bitcast_convert_type(x, I32)
    return bits ^ (lax.shift_right_arithmetic(bits, 31) & 0x7FFFFFFF)


def _two(x):
    return jnp.concatenate([x, x], axis=1)


def _sparse_attn_kernel(qi_ref, w_ref, q_ref, ki_ref, k_ref, v_ref, bias_ref, o_ref,
                        keys_ref, stage_ref, wb_ref, thr_ref, p_ref, m_ref, l_ref, al_ref, acc_ref,
                        *, tq, past, length, topk):
    it = pl.program_id(1)
    qpos0 = past + it * tq
    kt_diag = qpos0 // KEY_TILE
    n_far = jnp.maximum(kt_diag - 1, 0) // 2
    kt_near0 = 2 * n_far
    n_tiles = kt_diag + 1
    rb = min(32, tq)

    w = w_ref[0]
    for h in range(IDX_HEADS):
        wb_ref[h] = jnp.broadcast_to(w[:, h:h + 1], (tq, LANES))
    qi_all = qi_ref[0].reshape(IDX_HEADS * tq, IDX_DIM)

    lane_id = lax.broadcasted_iota(I32, (rb, LANES), 1)
    row_id = lax.broadcasted_iota(I32, (rb, LANES), 0)

    def row_limit(r):
        pos = qpos0 + r * rb + row_id
        return jnp.minimum((lax.shift_right_logical(pos, int(math.log2(CHUNK))) + 1) * CHUNK, length)

    def score_tiles(kt0, n_sub, masked):
        width = n_sub * KEY_TILE
        s0 = pl.multiple_of(kt0 * KEY_TILE, width)
        stage_ref[:, :width] = _dot_nt(qi_all, ki_ref[0, pl.ds(s0, width), :])
        for r in range(tq // rb):
            rows = slice(r * rb, (r + 1) * rb)
            for u in range(n_sub):
                cols = slice(u * KEY_TILE, (u + 1) * KEY_TILE)
                acc = jnp.zeros((rb, KEY_TILE), F32)
                for h in range(IDX_HEADS):
                    d = stage_ref[h * tq + r * rb:h * tq + (r + 1) * rb, cols]
                    acc = acc + jnp.maximum(d, 0.0) * wb_ref[h, rows, :]
                key = _sortable(acc)
                if masked:
                    vis = (s0 + u * KEY_TILE + lane_id) < row_limit(r)
                    key = jnp.where(vis, key, KEY_NEG_INF)
                keys_ref[kt0 + u, rows, :] = key

    def far_scores(c, carry):
        score_tiles(2 * c, 2, False)
        return carry

    lax.fori_loop(0, n_far, far_scores, 0)

    def near_scores(kt, carry):
        score_tiles(kt, 1, True)
        return carry

    lax.fori_loop(kt_near0, n_tiles, near_scores, 0)

    n_steps = (n_tiles + COUNT_TILES - 1) // COUNT_TILES

    def blank(kt, carry):
        keys_ref[kt] = jnp.full((tq, LANES), KEY_NEG_INF, I32)
        return carry

    lax.fori_loop(n_tiles, n_steps * COUNT_TILES, blank, 0)

    sb = min(64, tq)
    for r in range(tq // sb):
        rows = slice(r * sb, (r + 1) * sb)

        def bit_step(bi, thr):
            cand = thr ^ lax.shift_left(jnp.int32(1), 31 - bi)

            def count(j, cnt):
                for u in range(COUNT_TILES):
                    ge = keys_ref[j * COUNT_TILES + u, rows, :] >= cand
                    cnt = cnt + jnp.where(ge, 1, 0)
                return cnt

            cnt = lax.fori_loop(0, n_steps, count, jnp.zeros((sb, LANES), I32))
            total = jnp.sum(cnt, axis=1, keepdims=True)
            return jnp.where(total >= topk, cand, thr)

        thr = lax.fori_loop(0, 32, bit_step, jnp.full((sb, LANES), INT_MIN, I32))
        thr_ref[rows, :] = jnp.maximum(thr, KEY_NEG_INF + 1)

    rows4 = GROUP * tq
    for g in range(N_KV_HEADS):
        q4 = q_ref[0, g * GROUP:(g + 1) * GROUP].reshape(rows4, HEAD_DIM)
        lanes = slice(g * HEAD_DIM, (g + 1) * HEAD_DIM)
        m_ref[...] = jnp.full((rows4, LANES), NEG_BIG, F32)
        l_ref[...] = jnp.zeros((rows4, LANES), F32)
        acc_ref[...] = jnp.zeros((rows4, HEAD_DIM), F32)

        def attend(kt0, n_sub, bias_tile):
            width = n_sub * KEY_TILE
            s0 = pl.multiple_of(kt0 * KEY_TILE, width)
            kg = k_ref[0, pl.ds(s0, width), :][:, lanes]
            stage_ref[:rows4, :width] = _dot_nt(q4, kg)
            for r in range(tq // rb):
                rows = slice(r * rb, (r + 1) * rb)
                thr = thr_ref[rows, :]
                sel = [keys_ref[kt0 + u, rows, :] >= thr for u in range(n_sub)]
                for hh in range(GROUP):
                    hrows = slice(hh * tq + r * rb, hh * tq + (r + 1) * rb)
                    s = [stage_ref[hrows, u * KEY_TILE:(u + 1) * KEY_TILE] for u in range(n_sub)]
                    if bias_tile is not None:
                        s = [s[0] + bias_ref[g * GROUP + hh, bias_tile, rows, :]]
                    m_old = m_ref[hrows, :]
                    sm = [jnp.where(sel[u], s[u], NEG_BIG) for u in range(n_sub)]
                    smax = sm[0] if n_sub == 1 else jnp.maximum(sm[0], sm[1])
                    m_new = jnp.maximum(m_old, jnp.max(smax, axis=1, keepdims=True))
                    p = [jnp.where(sel[u], jnp.exp(s[u] - m_new), 0.0) for u in range(n_sub)]
                    psum = p[0] if n_sub == 1 else p[0] + p[1]
                    alpha = jnp.exp(m_old - m_new)
                    l_ref[hrows, :] = alpha * l_ref[hrows, :] + jnp.sum(psum, axis=1, keepdims=True)
                    m_ref[hrows, :] = m_new
                    al_ref[hrows, :] = alpha
                    for u in range(n_sub):
                        p_ref[hrows, u * KEY_TILE:(u + 1) * KEY_TILE] = p[u].astype(BF16)
            vg = v_ref[0, pl.ds(s0, width), :][:, lanes]
            pv = _dot(p_ref[:, :width], vg)
            acc_ref[...] = acc_ref[...] * al_ref[:, :HEAD_DIM] + pv

        def far_attend(c, carry):
            attend(2 * c, 2, None)
            return carry

        lax.fori_loop(0, n_far, far_attend, 0)

        def near_attend(kt, carry):
            attend(kt, 1, kt - (kt_diag - 2))
            return carry

        lax.fori_loop(kt_near0, n_tiles, near_attend, 0)

        out = acc_ref[...] / l_ref[:, :HEAD_DIM]
        for hh in range(GROUP):
            o_ref[0, g * GROUP + hh] = out[hh * tq:(hh + 1) * tq].astype(o_ref.dtype)


def _t5_bucket(rel):
    half = N_BUCKETS // 2
    max_exact = half // 2
    n = jnp.abs(rel)
    large = max_exact + (jnp.log(jnp.maximum(n, max_exact).astype(jnp.float32) / max_exact)
                         / math.log(MAX_DISTANCE / max_exact) * (half - max_exact)).astype(jnp.int32)
    large = jnp.minimum(large, half - 1)
    return jnp.where(rel > 0, half, 0) + jnp.where(n < max_exact, n, large)


def _near_bias(rel_bias, tq):
    t = jnp.arange(tq, dtype=I32)[:, None]
    s = jnp.arange(3 * KEY_TILE, dtype=I32)[None, :] - 2 * KEY_TILE
    bias = rel_bias[_t5_bucket(s - t)].astype(F32) - rel_bias[N_BUCKETS // 2 - 1].astype(F32)
    return bias.reshape(tq, 3, KEY_TILE, N_HEADS).transpose(3, 1, 0, 2)


def _sparse_attn(qi, w, q, ki, k, v, bias, *, tq, past, length, topk):
    b, _, t, _ = q.shape
    lp = k.shape[1]
    assert past % KEY_TILE == 0 and lp % (COUNT_TILES * KEY_TILE) == 0 and lp >= length
    assert tq % min(32, tq) == 0 and (t == tq or tq == KEY_TILE)
    kernel = functools.partial(_sparse_attn_kernel, tq=tq, past=past, length=length, topk=topk)
    once = dict(pipeline_mode=pl.Buffered(1))
    return pl.pallas_call(
        kernel,
        grid=(b, t // tq),
        in_specs=[
            pl.BlockSpec((1, IDX_HEADS, tq, IDX_DIM), lambda bi, i: (bi, 0, i, 0)),
            pl.BlockSpec((1, tq, IDX_HEADS), lambda bi, i: (bi, i, 0)),
            pl.BlockSpec((1, N_HEADS, tq, HEAD_DIM), lambda bi, i: (bi, 0, i, 0)),
            pl.BlockSpec((1, lp, IDX_DIM), lambda bi, i: (bi, 0, 0), **once),
            pl.BlockSpec((1, lp, KV_WIDTH), lambda bi, i: (bi, 0, 0), **once),
            pl.BlockSpec((1, lp, KV_WIDTH), lambda bi, i: (bi, 0, 0), **once),
            pl.BlockSpec(bias.shape, lambda bi, i: (0, 0, 0, 0), **once),
        ],
        out_specs=pl.BlockSpec((1, N_HEADS, tq, HEAD_DIM), lambda bi, i: (bi, 0, i, 0)),
        out_shape=jax.ShapeDtypeStruct((b, N_HEADS, t, HEAD_DIM), BF16),
        scratch_shapes=[
            pltpu.VMEM((lp // KEY_TILE, tq, LANES), I32),
            pltpu.VMEM((IDX_HEADS * tq, FAR_CHUNK), F32),
            pltpu.VMEM((IDX_HEADS, tq, LANES), F32),
            pltpu.VMEM((tq, LANES), I32),
            pltpu.VMEM((GROUP * tq, FAR_CHUNK), BF16),
            pltpu.VMEM((GROUP * tq, LANES), F32),
            pltpu.VMEM((GROUP * tq, LANES), F32),
            pltpu.VMEM((GROUP * tq, LANES), F32),
            pltpu.VMEM((GROUP * tq, HEAD_DIM), F32),
        ],
        compiler_params=_params(("arbitrary", "arbitrary")),
        name="sparse_attn",
    )(qi, w, q, ki, k, v, bias)


def _hgrn_kernel(rf_ref, rq_ref, ri_ref, rg_ref, lb_ref, gn_ref, s0_ref, tri_ref, trib_ref, ones_ref,
                 o_ref, st_ref, *, tb, chunk):
    @pl.when(pl.program_id(2) == 0)
    def _():
        st_ref[...] = s0_ref[...]

    lb = lb_ref[...]
    nsub = chunk // SUB
    row_id = lax.broadcasted_iota(I32, (SUB, REC_DK), 0)
    blk_r = lax.broadcasted_iota(I32, (chunk, chunk), 0) // SUB
    blk_c = lax.broadcasted_iota(I32, (chunk, chunk), 1) // SUB

    for c in range(tb // chunk):
        rows = slice(c * chunk, (c + 1) * chunk)
        f = lb + (1.0 - lb) * jax.nn.sigmoid(rf_ref[rows, :])
        logf = jnp.log(f)
        kk = 1.0 - f
        qq = _silu(rq_ref[rows, :])
        vv = ri_ref[rows, :]
        vv_b = vv.astype(BF16)
        b = _dot_sel_lhs(tri_ref[...], logf)
        bl = _dot_sel_lhs(trib_ref[...], logf)
        tot = [bl[(i + 1) * SUB - 1:(i + 1) * SUB, :] for i in range(nsub)]
        b_end = b[chunk - 1:chunk, :]
        st = st_ref[0, 0]

        qd = qq * jnp.exp(bl)
        kend = kk * jnp.exp(jnp.concatenate([tot[i] - bl[i * SUB:(i + 1) * SUB] for i in range(nsub)], axis=0))
        kend_b = kend.astype(BF16)

        a_off = jnp.zeros((chunk, chunk), F32)
        for dist in range(nsub - 1):
            parts = []
            for i in range(nsub):
                blk = qd[i * SUB:(i + 1) * SUB]
                if dist > 0:
                    if i - dist >= 0:
                        span = tot[i - dist]
                        for r in range(i - dist + 1, i):
                            span = span + tot[r]
                        blk = blk * jnp.exp(span)
                    else:
                        blk = jnp.zeros_like(blk)
                parts.append(blk)
            qdd = jnp.concatenate(parts, axis=0).astype(BF16)
            a_off = a_off + jnp.where(blk_r - blk_c - 1 == dist, _dot_nt(qdd, kend_b), 0.0)
        o = _dot(a_off.astype(BF16), vv_b)

        o = o + _dot_nt((qq * jnp.exp(b)).astype(BF16), st.astype(BF16))

        o_diag = []
        for i in range(nsub):
            sl = slice(i * SUB, (i + 1) * SUB)
            bli, qi_, ki_, vi = bl[sl], qq[sl], kk[sl], vv[sl]
            z = []
            for t in range(SUB):
                diff = jnp.where(row_id <= t, bli[t:t + 1, :] - bli, -jnp.inf)
                z.append(qi_[t:t + 1, :] * ki_ * jnp.exp(diff))
            zsum = _dot(jnp.concatenate(z, axis=0).astype(BF16), ones_ref[...])
            for t in range(SUB):
                o_diag.append(jnp.sum(zsum[t * SUB:(t + 1) * SUB] * vi, axis=0, keepdims=True))
        o = o + jnp.concatenate(o_diag, axis=0)

        tail = [None] * nsub
        run = jnp.zeros_like(tot[0])
        for i in range(nsub - 1, -1, -1):
            tail[i] = run
            run = run + tot[i]
        kdec = kend * jnp.exp(jnp.concatenate([jnp.broadcast_to(tail[i], (SUB, REC_DK)) for i in range(nsub)], axis=0))
        st_ref[0, 0] = st * jnp.exp(b_end) + _dot(vv.T.astype(BF16), kdec.astype(BF16))

        o_ref[rows, :] = (_rms_rows(o, gn_ref[...]) * _silu(rg_ref[rows, :])).astype(o_ref.dtype)


def _hgrn(proj, lb, gn, s0t, *, tb, chunk):
    b, t, _ = proj.shape
    nh = REC_HEADS
    tri = np.tril(np.ones((chunk, chunk), np.float32))
    blk = np.arange(chunk) // SUB
    trib = tri * (blk[:, None] == blk[None, :])
    kernel = functools.partial(_hgrn_kernel, tb=tb, chunk=chunk)
    col = lambda part: pl.BlockSpec((None, tb, REC_DK), lambda bi, h, c: (bi, c, part * nh + h))
    const = lambda a: pl.BlockSpec(a.shape, lambda bi, h, c: (0,) * a.ndim)
    tri, trib = jnp.asarray(tri, BF16), jnp.asarray(trib, BF16)
    ones = jnp.ones((REC_DK, REC_DV), BF16)
    return pl.pallas_call(
        kernel,
        grid=(b, nh, t // tb),
        in_specs=[
            col(0), col(1), col(2), col(3),
            pl.BlockSpec((1, REC_DK), lambda bi, h, c: (0, h)),
            const(gn),
            pl.BlockSpec((1, 1, REC_DV, REC_DK), lambda bi, h, c: (bi, h, 0, 0)),
            const(tri), const(trib), const(ones),
        ],
        out_specs=[
            pl.BlockSpec((None, tb, REC_DV), lambda bi, h, c: (bi, c, h)),
            pl.BlockSpec((1, 1, REC_DV, REC_DK), lambda bi, h, c: (bi, h, 0, 0)),
        ],
        out_shape=[
            jax.ShapeDtypeStruct((b, t, REC_WIDTH), BF16),
            jax.ShapeDtypeStruct((b, nh, REC_DV, REC_DK), F32),
        ],
        compiler_params=_params(("arbitrary", "arbitrary", "arbitrary")),
        name="hgrn2",
    )(proj, proj, proj, proj, lb, gn, s0t, tri, trib, ones)


def _merge_kernel(x_ref, oa_ref, ob_ref, ga_ref, gb_ref, wa_ref, wb_ref, wo_ref, o_ref, m_ref):
    @pl.when(pl.program_id(1) == 0)
    def _():
        ma = jax.nn.sigmoid(ga_ref[...]) * _dot(oa_ref[...], wa_ref[...])
        mb = jax.nn.sigmoid(gb_ref[...]) * _dot(ob_ref[...], wb_ref[...])
        m_ref[...] = (ma + mb).astype(BF16)

    o_ref[...] = x_ref[...] + _dot(m_ref[...], wo_ref[...])


def _merge(x, oa, ob, gates, wa, wb, wo, tm, tn):
    n, d = x.shape
    nd = d // tn
    once = dict(pipeline_mode=pl.Buffered(1))
    return pl.pallas_call(
        _merge_kernel,
        grid=(n // tm, nd),
        in_specs=[
            pl.BlockSpec((tm, tn), lambda i, j: (i, j)),
            pl.BlockSpec((tm, oa.shape[1]), lambda i, j: (i, 0)),
            pl.BlockSpec((tm, ob.shape[1]), lambda i, j: (i, 0)),
            pl.BlockSpec((tm, d), lambda i, j: (i, 0)),
            pl.BlockSpec((tm, d), lambda i, j: (i, 1)),
            pl.BlockSpec(wa.shape, lambda i, j: (0, 0), **once),
            pl.BlockSpec(wb.shape, lambda i, j: (0, 0), **once),
            pl.BlockSpec((d, tn), lambda i, j: (0, j)),
        ],
        out_specs=pl.BlockSpec((tm, tn), lambda i, j: (i, j)),
        out_shape=jax.ShapeDtypeStruct((n, d), F32),
        scratch_shapes=[pltpu.VMEM((tm, d), BF16)],
        compiler_params=_params(("arbitrary", "arbitrary")),
        name="merge",
    )(x, oa, ob, gates, gates, wa, wb, wo)


def _pick(n, pref):
    return pref if n % pref == 0 else n


def _layer(x, past_k, past_v, past_ki, s0, lw):
    bsz, t, d = x.shape
    n = bsz * t
    past = past_k.shape[1]
    length = past + t
    topk = min(TOPK_MAX, length // 4)
    tm = _pick(n, 512)
    x2 = x.reshape(n, d)

    x1 = _ffn(x2, lw["norm_ffa"], lw["w_ffa_in"], lw["w_ffa_out"], tm, 512)

    q, k, kb, v, vb, qi, ki, kib, wi = _attn_prep(
        x1, lw["norm_mix"], lw["w_attn_in"], lw["gain_q"], lw["gain_k"], tm)
    proj_rec = _norm_proj(x1, lw["norm_mix"], lw["w_rec_in"], tm, 1024, "proj_rec")
    gates = _norm_proj(x1, lw["norm_mix"], lw["w_gate_in"], tm, _pick(2 * d, 1024), "proj_gate")

    tq = min(KEY_TILE, t)
    lp = -(-length // (COUNT_TILES * KEY_TILE)) * (COUNT_TILES * KEY_TILE)

    def keys(past_x, new_x, width):
        full = jnp.concatenate([past_x.reshape(bsz, past, width).astype(BF16), new_x.reshape(bsz, t, width)], axis=1)
        return jnp.pad(full, ((0, 0), (0, lp - length), (0, 0)))

    heads = lambda a: a.reshape(bsz, t, N_HEADS, HEAD_DIM).transpose(0, 2, 1, 3)
    o_attn = _sparse_attn(
        heads(qi), wi.reshape(bsz, t, IDX_HEADS), heads(q),
        keys(past_ki, kib, IDX_DIM), keys(past_k, kb, KV_WIDTH), keys(past_v, vb, KV_WIDTH),
        _near_bias(lw["rel_bias"], tq), tq=tq, past=past, length=length, topk=topk)
    o_attn = o_attn.transpose(0, 2, 1, 3).reshape(n, ATTN_WIDTH)

    chunk = min(CHUNK, t)
    o_rec, st = _hgrn(proj_rec.reshape(bsz, t, 4 * REC_WIDTH), lw["lower_bound"], lw["hgrn_norm"],
                      s0.swapaxes(-1, -2), tb=_pick(t, 256), chunk=chunk)

    x2 = _merge(x1, o_attn, o_rec.reshape(n, REC_WIDTH), gates,
                lw["w_branch_attn"], lw["w_branch_rec"], lw["w_out"], tm, _pick(d, 512))
    y = _ffn(x2, lw["norm_ffb"], lw["w_ffb_in"], lw["w_ffb_out"], tm, 512)
    return (y.reshape(bsz, t, d), k.reshape(bsz, t, N_KV_HEADS, HEAD_DIM),
            v.reshape(bsz, t, N_KV_HEADS, HEAD_DIM), ki, st.swapaxes(-1, -2).astype(s0.dtype))


def _layer_weights(l, lower_bounds, norm_ffa, w_ffa_in, w_ffa_out, norm_mix, w_in, qk_gain_q, qk_gain_k,
                   rel_bias, hgrn_norm, w_branch_attn, w_branch_rec, w_out, norm_ffb, w_ffb_in, w_ffb_out):
    w = w_in[l]
    widths = (ATTN_WIDTH, KV_WIDTH, KV_WIDTH, IDX_HEADS * IDX_DIM, IDX_DIM, IDX_HEADS,
              REC_WIDTH, REC_WIDTH, REC_WIDTH, REC_WIDTH)
    splits = np.cumsum(widths)
    d = w.shape[0]
    pad = lambda a: jnp.pad(a, ((0, 0), (0, LANES - a.shape[1])))
    w_attn_in = jnp.concatenate(
        [w[:, :splits[3]], pad(w[:, splits[3]:splits[4]]), pad(w[:, splits[4]:splits[5]])], axis=1)
    row = lambda a: a.reshape(1, -1).astype(F32)
    return dict(
        norm_ffa=row(norm_ffa[l]), w_ffa_in=w_ffa_in[l].astype(BF16), w_ffa_out=w_ffa_out[l].astype(BF16),
        norm_mix=row(norm_mix[l]),
        w_attn_in=w_attn_in.astype(BF16),
        w_rec_in=w[:, splits[5]:splits[9]].astype(BF16),
        w_gate_in=w[:, splits[9]:splits[9] + 2 * d].astype(BF16),
        gain_q=row(jnp.tile(qk_gain_q[l], N_HEADS) * (HEAD_DIM ** -0.5)),
        gain_k=row(jnp.tile(qk_gain_k[l], N_KV_HEADS)),
        rel_bias=rel_bias,
        lower_bound=row(lower_bounds[l]),
        hgrn_norm=row(hgrn_norm[l]),
        w_branch_attn=w_branch_attn[l].astype(BF16), w_branch_rec=w_branch_rec[l].astype(BF16),
        w_out=w_out[l].astype(BF16),
        norm_ffb=row(norm_ffb[l]), w_ffb_in=w_ffb_in[l].astype(BF16), w_ffb_out=w_ffb_out[l].astype(BF16),
    )


def _trunk(x, past_k, past_v, past_ki, s0, layers):
    ks, vs, kis, ss = [], [], [], []
    for l, lw in enumerate(layers):
        x, k, v, ki, s = _layer(x, past_k[l], past_v[l], past_ki[l], s0[l], lw)
        ks.append(k)
        vs.append(v)
        kis.append(ki.reshape(k.shape[0], k.shape[1], IDX_DIM))
        ss.append(s)
    return x, jnp.stack(ks), jnp.stack(vs), jnp.stack(kis), jnp.stack(ss)


def kernel(x_prompt, x_sample, cache_k, cache_v, cache_kidx, state_hgrn, norm_ffa, w_ffa_in, w_ffa_out,
           norm_mix, w_in, qk_gain_q, qk_gain_k, rel_bias, hgrn_lb_raw, hgrn_norm, w_branch_attn,
           w_branch_rec, w_out, norm_ffb, w_ffb_in, w_ffb_out):
    depth = w_in.shape[0]
    lb_p = jax.nn.softmax(hgrn_lb_raw.astype(F32), axis=0)
    lower_bounds = jnp.cumsum(lb_p, axis=0)[:depth]
    layers = [_layer_weights(l, lower_bounds, norm_ffa, w_ffa_in, w_ffa_out, norm_mix, w_in, qk_gain_q,
                             qk_gain_k, rel_bias, hgrn_norm, w_branch_attn, w_branch_rec, w_out,
                             norm_ffb, w_ffb_in, w_ffb_out) for l in range(depth)]
    bp = x_prompt.shape[0]
    dt = x_prompt.dtype
    empty_k = jnp.zeros((depth, bp, 0, N_KV_HEADS, HEAD_DIM), dt)
    empty_ki = jnp.zeros((depth, bp, 0, IDX_DIM), dt)
    zero_s = jnp.zeros((depth, bp, REC_HEADS, REC_DK, REC_DV), state_hgrn.dtype)
    y_p, k_p, v_p, ki_p, s_p = _trunk(x_prompt, empty_k, empty_k, empty_ki, zero_s, layers)
    y_s, k_s, v_s, ki_s, s_s = _trunk(x_sample, cache_k, cache_v, cache_kidx, state_hgrn, layers)
    return (y_p, y_s, k_p, v_p, ki_p, s_p, k_s, v_s, ki_s, s_s)
```

```python
import functools
import math

import numpy as np
import jax
import jax.numpy as jnp
from jax import lax
from jax.experimental import pallas as pl
from jax.experimental.pallas import tpu as pltpu

F32 = jnp.float32
BF16 = jnp.bfloat16
I32 = jnp.int32

EPS = 1e-6
CHUNK = 64
HEAD_DIM = 64
N_HEADS = 16
N_KV_HEADS = 4
GROUP = N_HEADS // N_KV_HEADS
IDX_HEADS = 16
IDX_DIM = 64
TOPK_MAX = 256
N_BUCKETS = 32
MAX_DISTANCE = 128
REC_HEADS = 8
REC_DK = 128
REC_DV = 128
ATTN_WIDTH = N_HEADS * HEAD_DIM
KV_WIDTH = N_KV_HEADS * HEAD_DIM
REC_WIDTH = REC_HEADS * REC_DK

LANES = 128
SUBLANES = 8
KEY_TILE = LANES
SCORE_TILES = 2
COUNT_TILES = 4
ATTN_TILES = COUNT_TILES
V_ROWS = HEAD_DIM + 16
SUB = 16
VMEM_LIMIT_BYTES = 56 * 1024 * 1024

INT_MIN = -2 ** 31
KEY_NEG_INF = int(np.array(-np.inf, np.float32).view(np.int32)) ^ 0x7FFFFFFF
MASKED = -1e30
M_INIT = -1e29
LOG2E = math.log2(math.e)


def _params(sem):
    return pltpu.CompilerParams(dimension_semantics=sem, vmem_limit_bytes=VMEM_LIMIT_BYTES)


def _dot(a, b):
    return jnp.dot(a, b, preferred_element_type=F32)


def _dot_nt(a, b):
    return lax.dot_general(a, b, (((1,), (1,)), ((), ())), preferred_element_type=F32)


def _split3(x):
    x1 = x.astype(BF16)
    r1 = x - x1.astype(F32)
    x2 = r1.astype(BF16)
    x3 = (r1 - x2.astype(F32)).astype(BF16)
    return x1, x2, x3


def _dot_sel_rhs(x, sel):
    x1, x2, x3 = _split3(x)
    return _dot(x1, sel) + _dot(x2, sel) + _dot(x3, sel)


def _dot_sel_lhs(sel, x):
    x1, x2, x3 = _split3(x)
    return _dot(sel, x1) + _dot(sel, x2) + _dot(sel, x3)


def _rms_rows(x, g):
    ms = jnp.mean(x * x, axis=-1, keepdims=True)
    return x * lax.rsqrt(ms + EPS) * g


def _silu(x):
    return x * jax.nn.sigmoid(x)


def _ffn_kernel(x_ref, g_ref, wg_ref, wu_ref, wo_ref, o_ref, h_ref):
    @pl.when(pl.program_id(1) == 0)
    def _():
        x = x_ref[...]
        h_ref[...] = _rms_rows(x, g_ref[...]).astype(BF16)
        o_ref[...] = x

    h = h_ref[...]
    gate = _dot(h, wg_ref[...])
    up = _dot(h, wu_ref[...])
    a = (_silu(gate) * up * 0.5).astype(BF16)
    o_ref[...] += _dot(a, wo_ref[...])


def _ffn(x, g, w_in, w_out, tm, tf):
    n, d = x.shape
    dff = w_out.shape[0]
    nf = dff // tf
    return pl.pallas_call(
        _ffn_kernel,
        grid=(n // tm, nf),
        in_specs=[
            pl.BlockSpec((tm, d), lambda i, j: (i, 0)),
            pl.BlockSpec((1, d), lambda i, j: (0, 0)),
            pl.BlockSpec((d, tf), lambda i, j: (0, j)),
            pl.BlockSpec((d, tf), lambda i, j: (0, j + nf)),
            pl.BlockSpec((tf, d), lambda i, j: (j, 0)),
        ],
        out_specs=pl.BlockSpec((tm, d), lambda i, j: (i, 0)),
        out_shape=jax.ShapeDtypeStruct((n, d), F32),
        scratch_shapes=[pltpu.VMEM((tm, d), BF16)],
        compiler_params=_params(("arbitrary", "arbitrary")),
        name="ffn",
    )(x, g, w_in, w_in, w_out)


def _norm_proj_kernel(x_ref, g_ref, w_ref, o_ref, h_ref):
    @pl.when(pl.program_id(1) == 0)
    def _():
        h_ref[...] = _rms_rows(x_ref[...], g_ref[...]).astype(BF16)

    o_ref[...] = _dot(h_ref[...], w_ref[...]).astype(o_ref.dtype)


def _norm_proj(x, g, w, tm, tn, name):
    n, d = x.shape
    width = w.shape[1]
    return pl.pallas_call(
        _norm_proj_kernel,
        grid=(n // tm, width // tn),
        in_specs=[
            pl.BlockSpec((tm, d), lambda i, j: (i, 0)),
            pl.BlockSpec((1, d), lambda i, j: (0, 0)),
            pl.BlockSpec((d, tn), lambda i, j: (0, j)),
        ],
        out_specs=pl.BlockSpec((tm, tn), lambda i, j: (i, j)),
        out_shape=jax.ShapeDtypeStruct((n, width), F32),
        scratch_shapes=[pltpu.VMEM((tm, d), BF16)],
        compiler_params=_params(("arbitrary", "arbitrary")),
        name=name,
    )(x, g, w)


_C_Q = 0
_C_K = _C_Q + ATTN_WIDTH
_C_V = _C_K + KV_WIDTH
_C_QI = _C_V + KV_WIDTH
_C_KI = _C_QI + IDX_HEADS * IDX_DIM
_C_WI = _C_KI + LANES
_C_END = _C_WI + LANES


def _head_rms(x, gsum_ref, gexp_ref, gain_ref):
    ss = _dot_sel_rhs(x * x, gsum_ref[...])
    r = lax.rsqrt(ss * (1.0 / HEAD_DIM) + EPS)
    return x * _dot_sel_rhs(r, gexp_ref[...]) * gain_ref[...]


def _attn_prep_kernel(x_ref, g_ref, w_ref, gq_ref, gk_ref, sq_ref, eq_ref, sk_ref, ek_ref,
                      q_ref, k_ref, kb_ref, v_ref, vb_ref, qi_ref, ki_ref, kib_ref, wi_ref):
    h = _rms_rows(x_ref[...], g_ref[...]).astype(BF16)
    q = _dot(h, w_ref[:, _C_Q:_C_K])
    q_ref[...] = _head_rms(q, sq_ref, eq_ref, gq_ref).astype(BF16)
    k = _head_rms(_dot(h, w_ref[:, _C_K:_C_V]), sk_ref, ek_ref, gk_ref)
    k_ref[...] = k
    kb_ref[...] = k.astype(BF16)
    v = _dot(h, w_ref[:, _C_V:_C_QI])
    v_ref[...] = v
    vb_ref[...] = v.astype(BF16)
    qi_ref[...] = _dot(h, w_ref[:, _C_QI:_C_KI]).astype(BF16)
    ki = _dot(h, w_ref[:, _C_KI:_C_WI])[:, :IDX_DIM]
    ki_ref[...] = ki
    kib_ref[...] = ki.astype(BF16)
    wi_ref[...] = _dot(h, w_ref[:, _C_WI:_C_END])[:, :IDX_HEADS] * (IDX_HEADS ** -0.5)


def _head_selectors(n_heads):
    width = n_heads * HEAD_DIM
    col_head = np.arange(width) // HEAD_DIM
    gsum = (col_head[:, None] == np.arange(LANES)[None, :]).astype(np.float32)
    return jnp.asarray(gsum, BF16), jnp.asarray(gsum.T, BF16)


def _attn_prep(x, g, w, gq, gk, tm):
    n, d = x.shape
    sq, eq = _head_selectors(N_HEADS)
    sk, ek = _head_selectors(N_KV_HEADS)
    row = lambda width: pl.BlockSpec((tm, width), lambda i: (i, 0))
    full = lambda a: pl.BlockSpec(a.shape, lambda i: (0,) * a.ndim, pipeline_mode=pl.Buffered(1))
    out_shapes = [
        jax.ShapeDtypeStruct((n, ATTN_WIDTH), BF16),
        jax.ShapeDtypeStruct((n, KV_WIDTH), F32),
        jax.ShapeDtypeStruct((n, KV_WIDTH), BF16),
        jax.ShapeDtypeStruct((n, KV_WIDTH), F32),
        jax.ShapeDtypeStruct((n, KV_WIDTH), BF16),
        jax.ShapeDtypeStruct((n, IDX_HEADS * IDX_DIM), BF16),
        jax.ShapeDtypeStruct((n, IDX_DIM), F32),
        jax.ShapeDtypeStruct((n, IDX_DIM), BF16),
        jax.ShapeDtypeStruct((n, IDX_HEADS), F32),
    ]
    return pl.pallas_call(
        _attn_prep_kernel,
        grid=(n // tm,),
        in_specs=[row(d), full(g), full(w), full(gq), full(gk), full(sq), full(eq), full(sk), full(ek)],
        out_specs=[row(s.shape[1]) for s in out_shapes],
        out_shape=out_shapes,
        compiler_params=_params(("arbitrary",)),
        name="attn_prep",
    )(x, g, w, gq, gk, sq, eq, sk, ek)


def _sortable(x):
    bits = lax.bitcast_convert_type(x, I32)
    return bits ^ (lax.shift_right_arithmetic(bits, 31) & 0x7FFFFFFF)


def _sparse_attn_kernel(qi_ref, wt_ref, q_ref, ki_ref, k_ref, vt_ref, bias_ref, o_ref,
                        keys_ref, stage1_ref, stage3_ref, pt_ref, acc_ref, m_ref,
                        *, past, length, topk):
    tq = KEY_TILE
    it = pl.program_id(1)
    qpos0 = past + it * tq
    kt_diag = qpos0 // KEY_TILE
    n_tiles = kt_diag + 1
    rb = 32
    mb = 64

    qi_all = qi_ref[0].reshape(IDX_HEADS * tq, IDX_DIM)
    qcol = qpos0 + lax.broadcasted_iota(I32, (rb, tq), 1)
    col_limit = jnp.minimum((lax.shift_right_logical(qcol, int(math.log2(CHUNK))) + 1) * CHUNK, length)
    krow = lax.broadcasted_iota(I32, (rb, tq), 0)

    def score_step(kt0, n_sub, masked):
        width = n_sub * KEY_TILE
        s0 = pl.multiple_of(kt0 * KEY_TILE, width)
        stage1_ref[:width, :] = _dot_nt(ki_ref[0, pl.ds(s0, width), :], qi_all)
        for r in range(width // rb):
            rows = slice(r * rb, (r + 1) * rb)
            acc = jnp.zeros((rb, tq), F32)
            for h in range(IDX_HEADS):
                d = stage1_ref[rows, h * tq:(h + 1) * tq]
                acc = acc + jnp.maximum(d, 0.0) * wt_ref[0, h:h + 1, :]
            key = _sortable(acc)
            if masked:
                key = jnp.where(s0 + r * rb + krow < col_limit, key, KEY_NEG_INF)
            keys_ref[pl.ds(s0 + r * rb, rb), :] = key

    n_steps1 = (n_tiles + SCORE_TILES - 1) // SCORE_TILES

    def far_scores(c, carry):
        score_step(SCORE_TILES * c, SCORE_TILES, False)
        return carry

    lax.fori_loop(0, n_steps1 - 1, far_scores, 0)
    score_step(SCORE_TILES * (n_steps1 - 1), SCORE_TILES, True)

    n_steps = (n_tiles + COUNT_TILES - 1) // COUNT_TILES

    def blank(kt, carry):
        keys_ref[pl.ds(pl.multiple_of(kt * KEY_TILE, KEY_TILE), KEY_TILE), :] = jnp.full(
            (KEY_TILE, tq), KEY_NEG_INF, I32)
        return carry

    lax.fori_loop(n_steps1 * SCORE_TILES, n_steps * COUNT_TILES, blank, 0)

    step_keys = COUNT_TILES * KEY_TILE

    def bit_step(bi, thr):
        cand = thr ^ lax.shift_left(jnp.int32(1), 31 - bi)

        def count(j, cnt):
            blk = keys_ref[pl.ds(pl.multiple_of(j * step_keys, step_keys), step_keys), :]
            ones = jnp.where(blk >= cand, 1, 0)
            return cnt + jnp.sum(ones.reshape(step_keys // SUBLANES, SUBLANES, tq), axis=0)

        cnt = lax.fori_loop(0, n_steps, count, jnp.zeros((SUBLANES, tq), I32))
        total = jnp.sum(cnt, axis=0, keepdims=True)
        return jnp.where(total >= topk, cand, thr)

    thr = lax.fori_loop(0, 32, bit_step, jnp.full((1, tq), INT_MIN, I32))
    thr = jnp.maximum(thr, KEY_NEG_INF + 1)

    def to_mask(j, carry):
        sl = pl.ds(pl.multiple_of(j * step_keys, step_keys), step_keys)
        mask = jnp.where(keys_ref[sl, :] >= thr, 0.0, MASKED)
        keys_ref[sl, :] = lax.bitcast_convert_type(mask, I32)
        return carry

    lax.fori_loop(0, n_steps, to_mask, 0)

    cols4 = GROUP * tq
    width = ATTN_TILES * KEY_TILE
    acc_ref[...] = jnp.zeros(acc_ref.shape, F32)
    m_ref[...] = jnp.full(m_ref.shape, M_INIT, F32)

    def logits(c, g):
        s0 = pl.multiple_of(c * width, width)
        q4 = q_ref[0, g * GROUP:(g + 1) * GROUP].reshape(cols4, HEAD_DIM)
        kg = k_ref[0, pl.ds(s0, width), :][:, g * HEAD_DIM:(g + 1) * HEAD_DIM]
        stage3_ref[g] = _dot_nt(kg, q4)

    def softmax(c, g, with_bias):
        s0 = pl.multiple_of(c * width, width)
        cmax = [None] * GROUP
        for r in range(width // mb):
            rows = slice(r * mb, (r + 1) * mb)
            mask = lax.bitcast_convert_type(keys_ref[pl.ds(s0 + r * mb, mb), :], F32)
            if with_bias:
                u = (r * mb) // KEY_TILE
                bias_tile = jnp.clip(c * ATTN_TILES + u - (kt_diag - 2), 0, 2)
                brows = slice((r * mb) % KEY_TILE, (r * mb) % KEY_TILE + mb)
            for hh in range(GROUP):
                cols = slice(hh * tq, (hh + 1) * tq)
                s = stage3_ref[g, rows, cols] + mask
                if with_bias:
                    s = s + bias_ref[g * GROUP + hh, bias_tile, brows, :]
                stage3_ref[g, rows, cols] = s
                part = jnp.max(s.reshape(mb // SUBLANES, SUBLANES, tq), axis=0)
                cmax[hh] = part if r == 0 else jnp.maximum(cmax[hh], part)
        cmax = [jnp.max(cm, axis=0, keepdims=True) for cm in cmax]
        m_old = m_ref[g]
        m_new = jnp.maximum(m_old, jnp.concatenate(cmax, axis=1))
        m_ref[g] = m_new
        pt_ref[g] = jnp.exp2(stage3_ref[g] - m_new).astype(BF16)
        return jnp.exp2(m_old - m_new)

    def values(c, g, alpha):
        vt = jnp.concatenate([vt_ref[0, c * ATTN_TILES + u, g] for u in range(ATTN_TILES)], axis=1)
        acc_ref[g] = acc_ref[g] * alpha + _dot(vt, pt_ref[g])

    def attend_steps(c_lo, c_hi, with_bias):
        last = jnp.maximum(c_hi - 1, 0)
        logits(jnp.minimum(c_lo, last), 0)

        def body(c, carry):
            alpha = [None] * N_KV_HEADS
            for g in range(N_KV_HEADS):
                if g + 1 < N_KV_HEADS:
                    logits(c, g + 1)
                else:
                    logits(jnp.minimum(c + 1, last), 0)
                alpha[g] = softmax(c, g, with_bias)
                if g > 0:
                    values(c, g - 1, alpha[g - 1])
            values(c, N_KV_HEADS - 1, alpha[N_KV_HEADS - 1])
            return carry

        lax.fori_loop(c_lo, c_hi, body, 0)

    n_far3 = jnp.maximum(n_steps - 2, 0)
    attend_steps(0, n_far3, False)
    attend_steps(n_far3, n_steps, True)

    for g in range(N_KV_HEADS):
        acc = acc_ref[g]
        out_t = acc[:HEAD_DIM] / acc[HEAD_DIM:HEAD_DIM + 1]
        for pair in range(GROUP // 2):
            two = jnp.concatenate([out_t[:, (2 * pair) * tq:(2 * pair + 1) * tq],
                                   out_t[:, (2 * pair + 1) * tq:(2 * pair + 2) * tq]], axis=0)
            c0 = (g * GROUP + 2 * pair) * HEAD_DIM
            o_ref[0, :, c0:c0 + 2 * HEAD_DIM] = two.T.astype(o_ref.dtype)


def _t5_bucket(rel):
    half = N_BUCKETS // 2
    max_exact = half // 2
    n = jnp.abs(rel)
    large = max_exact + (jnp.log(jnp.maximum(n, max_exact).astype(jnp.float32) / max_exact)
                         / math.log(MAX_DISTANCE / max_exact) * (half - max_exact)).astype(jnp.int32)
    large = jnp.minimum(large, half - 1)
    return jnp.where(rel > 0, half, 0) + jnp.where(n < max_exact, n, large)


def _near_bias(rel_bias):
    t = jnp.arange(KEY_TILE, dtype=I32)[None, :]
    s = jnp.arange(3 * KEY_TILE, dtype=I32)[:, None] - 2 * KEY_TILE
    bias = rel_bias[_t5_bucket(s - t)].astype(F32) - rel_bias[N_BUCKETS // 2 - 1].astype(F32)
    return (bias * LOG2E).reshape(3, KEY_TILE, KEY_TILE, N_HEADS).transpose(3, 0, 1, 2)


def _sparse_attn(qi, wt, q, ki, k, vt, bias, *, past, length, topk):
    b, _, t, _ = q.shape
    lp = k.shape[1]
    tq = KEY_TILE
    assert past % KEY_TILE == 0 and t % tq == 0 and lp % (COUNT_TILES * KEY_TILE) == 0
    assert lp >= -(-(past + t) // (COUNT_TILES * KEY_TILE)) * COUNT_TILES * KEY_TILE
    kernel = functools.partial(_sparse_attn_kernel, past=past, length=length, topk=topk)
    once = dict(pipeline_mode=pl.Buffered(1))
    return pl.pallas_call(
        kernel,
        grid=(b, t // tq),
        in_specs=[
            pl.BlockSpec((1, IDX_HEADS, tq, IDX_DIM), lambda bi, i: (bi, 0, i, 0)),
            pl.BlockSpec((1, IDX_HEADS, tq), lambda bi, i: (bi, 0, i)),
            pl.BlockSpec((1, N_HEADS, tq, HEAD_DIM), lambda bi, i: (bi, 0, i, 0)),
            pl.BlockSpec((1, lp, IDX_DIM), lambda bi, i: (bi, 0, 0), **once),
            pl.BlockSpec((1, lp, KV_WIDTH), lambda bi, i: (bi, 0, 0), **once),
            pl.BlockSpec((1, lp // KEY_TILE, N_KV_HEADS, V_ROWS, KEY_TILE), lambda bi, i: (bi, 0, 0, 0, 0), **once),
            pl.BlockSpec(bias.shape, lambda bi, i: (0, 0, 0, 0), **once),
        ],
        out_specs=pl.BlockSpec((1, tq, ATTN_WIDTH), lambda bi, i: (bi, i, 0)),
        out_shape=jax.ShapeDtypeStruct((b, t, ATTN_WIDTH), BF16),
        scratch_shapes=[
            pltpu.VMEM((lp, tq), I32),
            pltpu.VMEM((SCORE_TILES * KEY_TILE, IDX_HEADS * tq), F32),
            pltpu.VMEM((N_KV_HEADS, ATTN_TILES * KEY_TILE, GROUP * tq), F32),
            pltpu.VMEM((N_KV_HEADS, ATTN_TILES * KEY_TILE, GROUP * tq), BF16),
            pltpu.VMEM((N_KV_HEADS, V_ROWS, GROUP * tq), F32),
            pltpu.VMEM((N_KV_HEADS, 1, GROUP * tq), F32),
        ],
        compiler_params=_params(("arbitrary", "arbitrary")),
        name="sparse_attn",
    )(qi, wt, q, ki, k, vt, bias)


def _hgrn_kernel(rf_ref, rq_ref, ri_ref, rg_ref, lb_ref, gn_ref, s0_ref, tri_ref, trib_ref, ones_ref,
                 o_ref, st_ref, *, tb, chunk):
    @pl.when(pl.program_id(2) == 0)
    def _():
        st_ref[...] = s0_ref[...]

    lb = lb_ref[...]
    nsub = chunk // SUB
    row_id = lax.broadcasted_iota(I32, (SUB, REC_DK), 0)
    blk_r = lax.broadcasted_iota(I32, (chunk, chunk), 0) // SUB
    blk_c = lax.broadcasted_iota(I32, (chunk, chunk), 1) // SUB

    for c in range(tb // chunk):
        rows = slice(c * chunk, (c + 1) * chunk)
        f = lb + (1.0 - lb) * jax.nn.sigmoid(rf_ref[rows, :])
        logf = jnp.log(f)
        kk = 1.0 - f
        qq = _silu(rq_ref[rows, :])
        vv = ri_ref[rows, :]
        vv_b = vv.astype(BF16)
        b = _dot_sel_lhs(tri_ref[...], logf)
        bl = _dot_sel_lhs(trib_ref[...], logf)
        tot = [bl[(i + 1) * SUB - 1:(i + 1) * SUB, :] for i in range(nsub)]
        b_end = b[chunk - 1:chunk, :]
        st = st_ref[0, 0]

        qd = qq * jnp.exp(bl)
        kend = kk * jnp.exp(jnp.concatenate([tot[i] - bl[i * SUB:(i + 1) * SUB] for i in range(nsub)], axis=0))
        kend_b = kend.astype(BF16)

        a_off = jnp.zeros((chunk, chunk), F32)
        for dist in range(nsub - 1):
            parts = []
            for i in range(nsub):
                blk = qd[i * SUB:(i + 1) * SUB]
                if dist > 0:
                    if i - dist >= 0:
                        span = tot[i - dist]
                        for r in range(i - dist + 1, i):
                            span = span + tot[r]
                        blk = blk * jnp.exp(span)
                    else:
                        blk = jnp.zeros_like(blk)
                parts.append(blk)
            qdd = jnp.concatenate(parts, axis=0).astype(BF16)
            a_off = a_off + jnp.where(blk_r - blk_c - 1 == dist, _dot_nt(qdd, kend_b), 0.0)
        o = _dot(a_off.astype(BF16), vv_b)

        o = o + _dot_nt((qq * jnp.exp(b)).astype(BF16), st.astype(BF16))

        o_diag = []
        for i in range(nsub):
            sl = slice(i * SUB, (i + 1) * SUB)
            bli, qi_, ki_, vi = bl[sl], qq[sl], kk[sl], vv[sl]
            z = []
            for t in range(SUB):
                diff = jnp.where(row_id <= t, bli[t:t + 1, :] - bli, -jnp.inf)
                z.append(qi_[t:t + 1, :] * ki_ * jnp.exp(diff))
            zsum = _dot(jnp.concatenate(z, axis=0).astype(BF16), ones_ref[...])
            for t in range(SUB):
                o_diag.append(jnp.sum(zsum[t * SUB:(t + 1) * SUB] * vi, axis=0, keepdims=True))
        o = o + jnp.concatenate(o_diag, axis=0)

        tail = [None] * nsub
        run = jnp.zeros_like(tot[0])
        for i in range(nsub - 1, -1, -1):
            tail[i] = run
            run = run + tot[i]
        kdec = kend * jnp.exp(jnp.concatenate([jnp.broadcast_to(tail[i], (SUB, REC_DK)) for i in range(nsub)], axis=0))
        st_ref[0, 0] = st * jnp.exp(b_end) + _dot(vv.T.astype(BF16), kdec.astype(BF16))

        o_ref[rows, :] = (_rms_rows(o, gn_ref[...]) * _silu(rg_ref[rows, :])).astype(o_ref.dtype)


def _hgrn(proj, lb, gn, s0t, *, tb, chunk):
    b, t, _ = proj.shape
    nh = REC_HEADS
    tri = np.tril(np.ones((chunk, chunk), np.float32))
    blk = np.arange(chunk) // SUB
    trib = tri * (blk[:, None] == blk[None, :])
    kernel = functools.partial(_hgrn_kernel, tb=tb, chunk=chunk)
    col = lambda part: pl.BlockSpec((None, tb, REC_DK), lambda bi, h, c: (bi, c, part * nh + h))
    const = lambda a: pl.BlockSpec(a.shape, lambda bi, h, c: (0,) * a.ndim)
    tri, trib = jnp.asarray(tri, BF16), jnp.asarray(trib, BF16)
    ones = jnp.ones((REC_DK, REC_DV), BF16)
    return pl.pallas_call(
        kernel,
        grid=(b, nh, t // tb),
        in_specs=[
            col(0), col(1), col(2), col(3),
            pl.BlockSpec((1, REC_DK), lambda bi, h, c: (0, h)),
            const(gn),
            pl.BlockSpec((1, 1, REC_DV, REC_DK), lambda bi, h, c: (bi, h, 0, 0)),
            const(tri), const(trib), const(ones),
        ],
        out_specs=[
            pl.BlockSpec((None, tb, REC_DV), lambda bi, h, c: (bi, c, h)),
            pl.BlockSpec((1, 1, REC_DV, REC_DK), lambda bi, h, c: (bi, h, 0, 0)),
        ],
        out_shape=[
            jax.ShapeDtypeStruct((b, t, REC_WIDTH), BF16),
            jax.ShapeDtypeStruct((b, nh, REC_DV, REC_DK), F32),
        ],
        compiler_params=_params(("arbitrary", "arbitrary", "arbitrary")),
        name="hgrn2",
    )(proj, proj, proj, proj, lb, gn, s0t, tri, trib, ones)


def _merge_kernel(x_ref, oa_ref, ob_ref, ga_ref, gb_ref, wa_ref, wb_ref, wo_ref, o_ref, m_ref):
    @pl.when(pl.program_id(1) == 0)
    def _():
        ma = jax.nn.sigmoid(ga_ref[...]) * _dot(oa_ref[...], wa_ref[...])
        mb = jax.nn.sigmoid(gb_ref[...]) * _dot(ob_ref[...], wb_ref[...])
        m_ref[...] = (ma + mb).astype(BF16)

    o_ref[...] = x_ref[...] + _dot(m_ref[...], wo_ref[...])


def _merge(x, oa, ob, gates, wa, wb, wo, tm, tn):
    n, d = x.shape
    nd = d // tn
    once = dict(pipeline_mode=pl.Buffered(1))
    return pl.pallas_call(
        _merge_kernel,
        grid=(n // tm, nd),
        in_specs=[
            pl.BlockSpec((tm, tn), lambda i, j: (i, j)),
            pl.BlockSpec((tm, oa.shape[1]), lambda i, j: (i, 0)),
            pl.BlockSpec((tm, ob.shape[1]), lambda i, j: (i, 0)),
            pl.BlockSpec((tm, d), lambda i, j: (i, 0)),
            pl.BlockSpec((tm, d), lambda i, j: (i, 1)),
            pl.BlockSpec(wa.shape, lambda i, j: (0, 0), **once),
            pl.BlockSpec(wb.shape, lambda i, j: (0, 0), **once),
            pl.BlockSpec((d, tn), lambda i, j: (0, j)),
        ],
        out_specs=pl.BlockSpec((tm, tn), lambda i, j: (i, j)),
        out_shape=jax.ShapeDtypeStruct((n, d), F32),
        scratch_shapes=[pltpu.VMEM((tm, d), BF16)],
        compiler_params=_params(("arbitrary", "arbitrary")),
        name="merge",
    )(x, oa, ob, gates, gates, wa, wb, wo)


def _pick(n, pref):
    return pref if n % pref == 0 else n


def _layer(x, past_k, past_v, past_ki, s0, lw):
    bsz, t, d = x.shape
    n = bsz * t
    past = past_k.shape[1]
    length = past + t
    topk = min(TOPK_MAX, length // 4)
    tm = _pick(n, 512)
    x2 = x.reshape(n, d)

    x1 = _ffn(x2, lw["norm_ffa"], lw["w_ffa_in"], lw["w_ffa_out"], tm, 512)

    q, k, kb, v, vb, qi, ki, kib, wi = _attn_prep(
        x1, lw["norm_mix"], lw["w_attn_in"], lw["gain_q"], lw["gain_k"], tm)
    proj_rec = _norm_proj(x1, lw["norm_mix"], lw["w_rec_in"], tm, 1024, "proj_rec")
    gates = _norm_proj(x1, lw["norm_mix"], lw["w_gate_in"], tm, _pick(2 * d, 1024), "proj_gate")

    tpad = -(-t // KEY_TILE) * KEY_TILE
    step = COUNT_TILES * KEY_TILE
    lp = -(-(past + tpad) // step) * step

    def keys(past_x, new_x, width):
        full = jnp.concatenate([past_x.reshape(bsz, past, width).astype(BF16), new_x.reshape(bsz, t, width)], axis=1)
        return jnp.pad(full, ((0, 0), (0, lp - length), (0, 0)))

    def heads(a):
        a = jnp.pad(a.reshape(bsz, t, N_HEADS, HEAD_DIM), ((0, 0), (0, tpad - t), (0, 0), (0, 0)))
        return a.transpose(0, 2, 1, 3)

    wt = jnp.pad(wi.reshape(bsz, t, IDX_HEADS), ((0, 0), (0, tpad - t), (0, 0))).transpose(0, 2, 1)
    v_all = keys(past_v, vb, KV_WIDTH).reshape(bsz, lp // KEY_TILE, KEY_TILE, N_KV_HEADS, HEAD_DIM)
    vt = jnp.concatenate(
        [v_all.transpose(0, 1, 3, 4, 2),
         jnp.ones((bsz, lp // KEY_TILE, N_KV_HEADS, V_ROWS - HEAD_DIM, KEY_TILE), BF16)], axis=3)
    o_attn = _sparse_attn(
        heads(qi), wt, heads(q), keys(past_ki, kib, IDX_DIM), keys(past_k, kb, KV_WIDTH), vt,
        _near_bias(lw["rel_bias"]), past=past, length=length, topk=topk)
    o_attn = o_attn[:, :t].reshape(n, ATTN_WIDTH)

    chunk = min(CHUNK, t)
    o_rec, st = _hgrn(proj_rec.reshape(bsz, t, 4 * REC_WIDTH), lw["lower_bound"], lw["hgrn_norm"],
                      s0.swapaxes(-1, -2), tb=_pick(t, 256), chunk=chunk)

    x2 = _merge(x1, o_attn, o_rec.reshape(n, REC_WIDTH), gates,
                lw["w_branch_attn"], lw["w_branch_rec"], lw["w_out"], tm, _pick(d, 512))
    y = _ffn(x2, lw["norm_ffb"], lw["w_ffb_in"], lw["w_ffb_out"], tm, 512)
    return (y.reshape(bsz, t, d), k.reshape(bsz, t, N_KV_HEADS, HEAD_DIM),
            v.reshape(bsz, t, N_KV_HEADS, HEAD_DIM), ki, st.swapaxes(-1, -2).astype(s0.dtype))


def _layer_weights(l, lower_bounds, norm_ffa, w_ffa_in, w_ffa_out, norm_mix, w_in, qk_gain_q, qk_gain_k,
                   rel_bias, hgrn_norm, w_branch_attn, w_branch_rec, w_out, norm_ffb, w_ffb_in, w_ffb_out):
    w = w_in[l]
    widths = (ATTN_WIDTH, KV_WIDTH, KV_WIDTH, IDX_HEADS * IDX_DIM, IDX_DIM, IDX_HEADS,
              REC_WIDTH, REC_WIDTH, REC_WIDTH, REC_WIDTH)
    splits = np.cumsum(widths)
    d = w.shape[0]
    pad = lambda a: jnp.pad(a, ((0, 0), (0, LANES - a.shape[1])))
    w_attn_in = jnp.concatenate(
        [w[:, :splits[3]], pad(w[:, splits[3]:splits[4]]), pad(w[:, splits[4]:splits[5]])], axis=1)
    row = lambda a: a.reshape(1, -1).astype(F32)
    return dict(
        norm_ffa=row(norm_ffa[l]), w_ffa_in=w_ffa_in[l].astype(BF16), w_ffa_out=w_ffa_out[l].astype(BF16),
        norm_mix=row(norm_mix[l]),
        w_attn_in=w_attn_in.astype(BF16),
        w_rec_in=w[:, splits[5]:splits[9]].astype(BF16),
        w_gate_in=w[:, splits[9]:splits[9] + 2 * d].astype(BF16),
        gain_q=row(jnp.tile(qk_gain_q[l], N_HEADS) * (HEAD_DIM ** -0.5 * LOG2E)),
        gain_k=row(jnp.tile(qk_gain_k[l], N_KV_HEADS)),
        rel_bias=rel_bias,
        lower_bound=row(lower_bounds[l]),
        hgrn_norm=row(hgrn_norm[l]),
        w_branch_attn=w_branch_attn[l].astype(BF16), w_branch_rec=w_branch_rec[l].astype(BF16),
        w_out=w_out[l].astype(BF16),
        norm_ffb=row(norm_ffb[l]), w_ffb_in=w_ffb_in[l].astype(BF16), w_ffb_out=w_ffb_out[l].astype(BF16),
    )


def _trunk(x, past_k, past_v, past_ki, s0, layers):
    ks, vs, kis, ss = [], [], [], []
    for l, lw in enumerate(layers):
        x, k, v, ki, s = _layer(x, past_k[l], past_v[l], past_ki[l], s0[l], lw)
        ks.append(k)
        vs.append(v)
        kis.append(ki.reshape(k.shape[0], k.shape[1], IDX_DIM))
        ss.append(s)
    return x, jnp.stack(ks), jnp.stack(vs), jnp.stack(kis), jnp.stack(ss)


def kernel(x_prompt, x_sample, cache_k, cache_v, cache_kidx, state_hgrn, norm_ffa, w_ffa_in, w_ffa_out,
           norm_mix, w_in, qk_gain_q, qk_gain_k, rel_bias, hgrn_lb_raw, hgrn_norm, w_branch_attn,
           w_branch_rec, w_out, norm_ffb, w_ffb_in, w_ffb_out):
    depth = w_in.shape[0]
    lb_p = jax.nn.softmax(hgrn_lb_raw.astype(F32), axis=0)
    lower_bounds = jnp.cumsum(lb_p, axis=0)[:depth]
    layers = [_layer_weights(l, lower_bounds, norm_ffa, w_ffa_in, w_ffa_out, norm_mix, w_in, qk_gain_q,
                             qk_gain_k, rel_bias, hgrn_norm, w_branch_attn, w_branch_rec, w_out,
                             norm_ffb, w_ffb_in, w_ffb_out) for l in range(depth)]
    bp = x_prompt.shape[0]
    dt = x_prompt.dtype
    empty_k = jnp.zeros((depth, bp, 0, N_KV_HEADS, HEAD_DIM), dt)
    empty_ki = jnp.zeros((depth, bp, 0, IDX_DIM), dt)
    zero_s = jnp.zeros((depth, bp, REC_HEADS, REC_DK, REC_DV), state_hgrn.dtype)
    y_p, k_p, v_p, ki_p, s_p = _trunk(x_prompt, empty_k, empty_k, empty_ki, zero_s, layers)
    y_s, k_s, v_s, ki_s, s_s = _trunk(x_sample, cache_k, cache_v, cache_kidx, state_hgrn, layers)
    return (y_p, y_s, k_p, v_p, ki_p, s_p, k_s, v_s, ki_s, s_s)
```

```python
import functools
import math

import numpy as np
import jax
import jax.numpy as jnp
from jax import lax
from jax.experimental import pallas as pl
from jax.experimental.pallas import tpu as pltpu

F32 = jnp.float32
BF16 = jnp.bfloat16
I32 = jnp.int32

EPS = 1e-6
CHUNK = 64
HEAD_DIM = 64
N_HEADS = 16
N_KV_HEADS = 4
GROUP = N_HEADS // N_KV_HEADS
IDX_HEADS = 16
IDX_DIM = 64
TOPK_MAX = 256
N_BUCKETS = 32
MAX_DISTANCE = 128
REC_HEADS = 8
REC_DK = 128
REC_DV = 128
ATTN_WIDTH = N_HEADS * HEAD_DIM
KV_WIDTH = N_KV_HEADS * HEAD_DIM
REC_WIDTH = REC_HEADS * REC_DK

LANES = 128
SUBLANES = 8
KEY_TILE = LANES
SCORE_TILES = 2
COUNT_TILES = 4
ATTN_TILES = COUNT_TILES
V_ROWS = HEAD_DIM + 16
SUB = 16
VMEM_LIMIT_BYTES = 56 * 1024 * 1024

INT_MIN = -2 ** 31
KEY_NEG_INF = int(np.array(-np.inf, np.float32).view(np.int32)) ^ 0x7FFFFFFF
MASKED = -1e30
M_INIT = -1e29
LOG2E = math.log2(math.e)


def _params(sem):
    return pltpu.CompilerParams(dimension_semantics=sem, vmem_limit_bytes=VMEM_LIMIT_BYTES)


def _dot(a, b):
    return jnp.dot(a, b, preferred_element_type=F32)


def _dot_nt(a, b):
    return lax.dot_general(a, b, (((1,), (1,)), ((), ())), preferred_element_type=F32)


def _split3(x):
    x1 = x.astype(BF16)
    r1 = x - x1.astype(F32)
    x2 = r1.astype(BF16)
    x3 = (r1 - x2.astype(F32)).astype(BF16)
    return x1, x2, x3


def _dot_sel_rhs(x, sel):
    x1, x2, x3 = _split3(x)
    return _dot(x1, sel) + _dot(x2, sel) + _dot(x3, sel)


def _dot_sel_lhs(sel, x):
    x1, x2, x3 = _split3(x)
    return _dot(sel, x1) + _dot(sel, x2) + _dot(sel, x3)


def _rms_rows(x, g):
    ms = jnp.mean(x * x, axis=-1, keepdims=True)
    return x * lax.rsqrt(ms + EPS) * g


def _silu(x):
    return x * jax.nn.sigmoid(x)


def _ffn_kernel(x_ref, g_ref, wg_ref, wu_ref, wo_ref, o_ref, h_ref):
    @pl.when(pl.program_id(1) == 0)
    def _():
        x = x_ref[...]
        h_ref[...] = _rms_rows(x, g_ref[...]).astype(BF16)
        o_ref[...] = x

    h = h_ref[...]
    gate = _dot(h, wg_ref[...])
    up = _dot(h, wu_ref[...])
    a = (_silu(gate) * up * 0.5).astype(BF16)
    o_ref[...] += _dot(a, wo_ref[...])


def _ffn(x, g, w_in, w_out, tm, tf):
    n, d = x.shape
    dff = w_out.shape[0]
    nf = dff // tf
    return pl.pallas_call(
        _ffn_kernel,
        grid=(n // tm, nf),
        in_specs=[
            pl.BlockSpec((tm, d), lambda i, j: (i, 0)),
            pl.BlockSpec((1, d), lambda i, j: (0, 0)),
            pl.BlockSpec((d, tf), lambda i, j: (0, j)),
            pl.BlockSpec((d, tf), lambda i, j: (0, j + nf)),
            pl.BlockSpec((tf, d), lambda i, j: (j, 0)),
        ],
        out_specs=pl.BlockSpec((tm, d), lambda i, j: (i, 0)),
        out_shape=jax.ShapeDtypeStruct((n, d), F32),
        scratch_shapes=[pltpu.VMEM((tm, d), BF16)],
        compiler_params=_params(("arbitrary", "arbitrary")),
        name="ffn",
    )(x, g, w_in, w_in, w_out)


def _norm_proj_kernel(x_ref, g_ref, w_ref, o_ref, h_ref):
    @pl.when(pl.program_id(1) == 0)
    def _():
        h_ref[...] = _rms_rows(x_ref[...], g_ref[...]).astype(BF16)

    o_ref[...] = _dot(h_ref[...], w_ref[...]).astype(o_ref.dtype)


def _norm_proj(x, g, w, tm, tn, name):
    n, d = x.shape
    width = w.shape[1]
    return pl.pallas_call(
        _norm_proj_kernel,
        grid=(n // tm, width // tn),
        in_specs=[
            pl.BlockSpec((tm, d), lambda i, j: (i, 0)),
            pl.BlockSpec((1, d), lambda i, j: (0, 0)),
            pl.BlockSpec((d, tn), lambda i, j: (0, j)),
        ],
        out_specs=pl.BlockSpec((tm, tn), lambda i, j: (i, j)),
        out_shape=jax.ShapeDtypeStruct((n, width), F32),
        scratch_shapes=[pltpu.VMEM((tm, d), BF16)],
        compiler_params=_params(("arbitrary", "arbitrary")),
        name=name,
    )(x, g, w)


_C_Q = 0
_C_K = _C_Q + ATTN_WIDTH
_C_V = _C_K + KV_WIDTH
_C_QI = _C_V + KV_WIDTH
_C_KI = _C_QI + IDX_HEADS * IDX_DIM
_C_WI = _C_KI + LANES
_C_END = _C_WI + LANES


def _head_rms(x, gsum_ref, gexp_ref, gain_ref):
    ss = _dot_sel_rhs(x * x, gsum_ref[...])
    r = lax.rsqrt(ss * (1.0 / HEAD_DIM) + EPS)
    return x * _dot_sel_rhs(r, gexp_ref[...]) * gain_ref[...]


def _attn_prep_kernel(x_ref, g_ref, w_ref, gq_ref, gk_ref, sq_ref, eq_ref, sk_ref, ek_ref,
                      q_ref, k_ref, kb_ref, v_ref, vb_ref, qi_ref, ki_ref, kib_ref, wi_ref):
    h = _rms_rows(x_ref[...], g_ref[...]).astype(BF16)
    q = _dot(h, w_ref[:, _C_Q:_C_K])
    q_ref[...] = _head_rms(q, sq_ref, eq_ref, gq_ref).astype(BF16)
    k = _head_rms(_dot(h, w_ref[:, _C_K:_C_V]), sk_ref, ek_ref, gk_ref)
    k_ref[...] = k
    kb_ref[...] = k.astype(BF16)
    v = _dot(h, w_ref[:, _C_V:_C_QI])
    v_ref[...] = v
    vb_ref[...] = v.astype(BF16)
    qi_ref[...] = _dot(h, w_ref[:, _C_QI:_C_KI]).astype(BF16)
    ki = _dot(h, w_ref[:, _C_KI:_C_WI])[:, :IDX_DIM]
    ki_ref[...] = ki
    kib_ref[...] = ki.astype(BF16)
    wi_ref[...] = _dot(h, w_ref[:, _C_WI:_C_END])[:, :IDX_HEADS] * (IDX_HEADS ** -0.5)


def _head_selectors(n_heads):
    width = n_heads * HEAD_DIM
    col_head = np.arange(width) // HEAD_DIM
    gsum = (col_head[:, None] == np.arange(LANES)[None, :]).astype(np.float32)
    return jnp.asarray(gsum, BF16), jnp.asarray(gsum.T, BF16)


def _attn_prep(x, g, w, gq, gk, tm):
    n, d = x.shape
    sq, eq = _head_selectors(N_HEADS)
    sk, ek = _head_selectors(N_KV_HEADS)
    row = lambda width: pl.BlockSpec((tm, width), lambda i: (i, 0))
    full = lambda a: pl.BlockSpec(a.shape, lambda i: (0,) * a.ndim, pipeline_mode=pl.Buffered(1))
    out_shapes = [
        jax.ShapeDtypeStruct((n, ATTN_WIDTH), BF16),
        jax.ShapeDtypeStruct((n, KV_WIDTH), F32),
        jax.ShapeDtypeStruct((n, KV_WIDTH), BF16),
        jax.ShapeDtypeStruct((n, KV_WIDTH), F32),
        jax.ShapeDtypeStruct((n, KV_WIDTH), BF16),
        jax.ShapeDtypeStruct((n, IDX_HEADS * IDX_DIM), BF16),
        jax.ShapeDtypeStruct((n, IDX_DIM), F32),
        jax.ShapeDtypeStruct((n, IDX_DIM), BF16),
        jax.ShapeDtypeStruct((n, IDX_HEADS), F32),
    ]
    return pl.pallas_call(
        _attn_prep_kernel,
        grid=(n // tm,),
        in_specs=[row(d), full(g), full(w), full(gq), full(gk), full(sq), full(eq), full(sk), full(ek)],
        out_specs=[row(s.shape[1]) for s in out_shapes],
        out_shape=out_shapes,
        compiler_params=_params(("arbitrary",)),
        name="attn_prep",
    )(x, g, w, gq, gk, sq, eq, sk, ek)


def _sortable(x):
    bits = lax.bitcast_convert_type(x, I32)
    return bits ^ (lax.shift_right_arithmetic(bits, 31) & 0x7FFFFFFF)


def _sparse_attn_kernel(qi_ref, wt_ref, q_ref, ki_ref, k_ref, vt_ref, bias_ref, o_ref,
                        keys_ref, stage1_ref, stage3_ref, pt_ref, acc_ref, m_ref,
                        *, past, length, topk):
    tq = KEY_TILE
    it = pl.program_id(1)
    qpos0 = past + it * tq
    kt_diag = qpos0 // KEY_TILE
    n_tiles = kt_diag + 1
    rb = 32
    mb = 64

    qi_all = qi_ref[0].reshape(IDX_HEADS * tq, IDX_DIM)
    qcol = qpos0 + lax.broadcasted_iota(I32, (rb, tq), 1)
    col_limit = jnp.minimum((lax.shift_right_logical(qcol, int(math.log2(CHUNK))) + 1) * CHUNK, length)
    krow = lax.broadcasted_iota(I32, (rb, tq), 0)

    def score_step(kt0, n_sub, masked):
        width = n_sub * KEY_TILE
        s0 = pl.multiple_of(kt0 * KEY_TILE, width)
        stage1_ref[:width, :] = _dot_nt(ki_ref[0, pl.ds(s0, width), :], qi_all)
        for r in range(width // rb):
            rows = slice(r * rb, (r + 1) * rb)
            acc = jnp.zeros((rb, tq), F32)
            for h in range(IDX_HEADS):
                d = stage1_ref[rows, h * tq:(h + 1) * tq]
                acc = acc + jnp.maximum(d, 0.0) * wt_ref[0, h:h + 1, :]
            key = _sortable(acc)
            if masked:
                key = jnp.where(s0 + r * rb + krow < col_limit, key, KEY_NEG_INF)
            keys_ref[pl.ds(s0 + r * rb, rb), :] = key

    n_steps1 = (n_tiles + SCORE_TILES - 1) // SCORE_TILES

    def far_scores(c, carry):
        score_step(SCORE_TILES * c, SCORE_TILES, False)
        return carry

    lax.fori_loop(0, n_steps1 - 1, far_scores, 0)
    score_step(SCORE_TILES * (n_steps1 - 1), SCORE_TILES, True)

    n_steps = (n_tiles + COUNT_TILES - 1) // COUNT_TILES

    def blank(kt, carry):
        keys_ref[pl.ds(pl.multiple_of(kt * KEY_TILE, KEY_TILE), KEY_TILE), :] = jnp.full(
            (KEY_TILE, tq), KEY_NEG_INF, I32)
        return carry

    lax.fori_loop(n_steps1 * SCORE_TILES, n_steps * COUNT_TILES, blank, 0)

    step_keys = COUNT_TILES * KEY_TILE

    def bit_step(bi, thr):
        cand = thr ^ lax.shift_left(jnp.int32(1), 31 - bi)

        def count(j, cnt):
            blk = keys_ref[pl.ds(pl.multiple_of(j * step_keys, step_keys), step_keys), :]
            ones = jnp.where(blk >= cand, 1, 0)
            return cnt + jnp.sum(ones.reshape(step_keys // SUBLANES, SUBLANES, tq), axis=0)

        cnt = lax.fori_loop(0, n_steps, count, jnp.zeros((SUBLANES, tq), I32))
        total = jnp.sum(cnt, axis=0, keepdims=True)
        return jnp.where(total >= topk, cand, thr)

    thr = lax.fori_loop(0, 32, bit_step, jnp.full((1, tq), INT_MIN, I32))
    thr = jnp.maximum(thr, KEY_NEG_INF + 1)

    def to_mask(j, carry):
        sl = pl.ds(pl.multiple_of(j * step_keys, step_keys), step_keys)
        mask = jnp.where(keys_ref[sl, :] >= thr, 0.0, MASKED)
        keys_ref[sl, :] = lax.bitcast_convert_type(mask, I32)
        return carry

    lax.fori_loop(0, n_steps, to_mask, 0)

    cols4 = GROUP * tq
    width = ATTN_TILES * KEY_TILE
    acc_ref[...] = jnp.zeros(acc_ref.shape, F32)
    m_ref[...] = jnp.full(m_ref.shape, M_INIT, F32)

    def logits(c, g):
        s0 = pl.multiple_of(c * width, width)
        q4 = q_ref[0, g * GROUP:(g + 1) * GROUP].reshape(cols4, HEAD_DIM)
        kg = k_ref[0, pl.ds(s0, width), :][:, g * HEAD_DIM:(g + 1) * HEAD_DIM]
        stage3_ref[g] = _dot_nt(kg, q4)

    def softmax(c, g, with_bias):
        s0 = pl.multiple_of(c * width, width)
        cmax = [None] * GROUP
        for r in range(width // mb):
            rows = slice(r * mb, (r + 1) * mb)
            mask = lax.bitcast_convert_type(keys_ref[pl.ds(s0 + r * mb, mb), :], F32)
            if with_bias:
                u = (r * mb) // KEY_TILE
                bias_tile = jnp.clip(c * ATTN_TILES + u - (kt_diag - 2), 0, 2)
                brows = slice((r * mb) % KEY_TILE, (r * mb) % KEY_TILE + mb)
            for hh in range(GROUP):
                cols = slice(hh * tq, (hh + 1) * tq)
                s = stage3_ref[g, rows, cols] + mask
                if with_bias:
                    s = s + bias_ref[g * GROUP + hh, bias_tile, brows, :]
                stage3_ref[g, rows, cols] = s
                part = jnp.max(s.reshape(mb // SUBLANES, SUBLANES, tq), axis=0)
                cmax[hh] = part if r == 0 else jnp.maximum(cmax[hh], part)
        cmax = [jnp.max(cm, axis=0, keepdims=True) for cm in cmax]
        m_old = m_ref[g]
        m_new = jnp.maximum(m_old, jnp.concatenate(cmax, axis=1))
        m_ref[g] = m_new
        for r in range(width // rb):
            rows = slice(r * rb, (r + 1) * rb)
            pt_ref[g, rows, :] = jnp.exp2((stage3_ref[g, rows, :] - m_new).astype(BF16))
        return jnp.exp2(m_old - m_new)

    def values(c, g, alpha):
        vt = jnp.concatenate([vt_ref[0, c * ATTN_TILES + u, g] for u in range(ATTN_TILES)], axis=1)
        acc_ref[g] = acc_ref[g] * alpha + _dot(vt, pt_ref[g])

    g_last = N_KV_HEADS - 1

    def attend_steps(c_lo, c_hi, with_bias):
        last = jnp.maximum(c_hi - 1, 0)
        logits(jnp.minimum(c_lo, last), 0)
        pt_ref[g_last] = jnp.zeros(pt_ref.shape[1:], BF16)

        def body(c, alpha_prev):
            for g in range(N_KV_HEADS):
                if g < g_last:
                    logits(c, g + 1)
                else:
                    logits(jnp.minimum(c + 1, last), 0)
                alpha = softmax(c, g, with_bias)
                if g == 0:
                    values(jnp.maximum(c - 1, 0), g_last, alpha_prev)
                else:
                    values(c, g - 1, alpha_prev)
                alpha_prev = alpha
            return alpha_prev

        alpha_last = lax.fori_loop(c_lo, c_hi, body, jnp.ones((1, cols4), F32))
        values(last, g_last, alpha_last)

    n_far3 = jnp.maximum(n_steps - 2, 0)
    attend_steps(0, n_far3, False)
    attend_steps(n_far3, n_steps, True)

    for g in range(N_KV_HEADS):
        acc = acc_ref[g]
        out_t = acc[:HEAD_DIM] / acc[HEAD_DIM:HEAD_DIM + 1]
        for pair in range(GROUP // 2):
            two = jnp.concatenate([out_t[:, (2 * pair) * tq:(2 * pair + 1) * tq],
                                   out_t[:, (2 * pair + 1) * tq:(2 * pair + 2) * tq]], axis=0)
            c0 = (g * GROUP + 2 * pair) * HEAD_DIM
            o_ref[0, :, c0:c0 + 2 * HEAD_DIM] = two.T.astype(o_ref.dtype)


def _t5_bucket(rel):
    half = N_BUCKETS // 2
    max_exact = half // 2
    n = jnp.abs(rel)
    large = max_exact + (jnp.log(jnp.maximum(n, max_exact).astype(jnp.float32) / max_exact)
                         / math.log(MAX_DISTANCE / max_exact) * (half - max_exact)).astype(jnp.int32)
    large = jnp.minimum(large, half - 1)
    return jnp.where(rel > 0, half, 0) + jnp.where(n < max_exact, n, large)


def _near_bias(rel_bias):
    t = jnp.arange(KEY_TILE, dtype=I32)[None, :]
    s = jnp.arange(3 * KEY_TILE, dtype=I32)[:, None] - 2 * KEY_TILE
    bias = rel_bias[_t5_bucket(s - t)].astype(F32) - rel_bias[N_BUCKETS // 2 - 1].astype(F32)
    return (bias * LOG2E).reshape(3, KEY_TILE, KEY_TILE, N_HEADS).transpose(3, 0, 1, 2)


def _sparse_attn(qi, wt, q, ki, k, vt, bias, *, past, length, topk):
    b, _, t, _ = q.shape
    lp = k.shape[1]
    tq = KEY_TILE
    assert past % KEY_TILE == 0 and t % tq == 0 and lp % (COUNT_TILES * KEY_TILE) == 0
    assert lp >= -(-(past + t) // (COUNT_TILES * KEY_TILE)) * COUNT_TILES * KEY_TILE
    kernel = functools.partial(_sparse_attn_kernel, past=past, length=length, topk=topk)
    once = dict(pipeline_mode=pl.Buffered(1))
    return pl.pallas_call(
        kernel,
        grid=(b, t // tq),
        in_specs=[
            pl.BlockSpec((1, IDX_HEADS, tq, IDX_DIM), lambda bi, i: (bi, 0, i, 0)),
            pl.BlockSpec((1, IDX_HEADS, tq), lambda bi, i: (bi, 0, i)),
            pl.BlockSpec((1, N_HEADS, tq, HEAD_DIM), lambda bi, i: (bi, 0, i, 0)),
            pl.BlockSpec((1, lp, IDX_DIM), lambda bi, i: (bi, 0, 0), **once),
            pl.BlockSpec((1, lp, KV_WIDTH), lambda bi, i: (bi, 0, 0), **once),
            pl.BlockSpec((1, lp // KEY_TILE, N_KV_HEADS, V_ROWS, KEY_TILE), lambda bi, i: (bi, 0, 0, 0, 0), **once),
            pl.BlockSpec(bias.shape, lambda bi, i: (0, 0, 0, 0), **once),
        ],
        out_specs=pl.BlockSpec((1, tq, ATTN_WIDTH), lambda bi, i: (bi, i, 0)),
        out_shape=jax.ShapeDtypeStruct((b, t, ATTN_WIDTH), BF16),
        scratch_shapes=[
            pltpu.VMEM((lp, tq), I32),
            pltpu.VMEM((SCORE_TILES * KEY_TILE, IDX_HEADS * tq), F32),
            pltpu.VMEM((N_KV_HEADS, ATTN_TILES * KEY_TILE, GROUP * tq), F32),
            pltpu.VMEM((N_KV_HEADS, ATTN_TILES * KEY_TILE, GROUP * tq), BF16),
            pltpu.VMEM((N_KV_HEADS, V_ROWS, GROUP * tq), F32),
            pltpu.VMEM((N_KV_HEADS, 1, GROUP * tq), F32),
        ],
        compiler_params=_params(("arbitrary", "arbitrary")),
        name="sparse_attn",
    )(qi, wt, q, ki, k, vt, bias)


def _hgrn_kernel(rf_ref, rq_ref, ri_ref, rg_ref, lb_ref, gn_ref, s0_ref, tri_ref, trib_ref, ones_ref,
                 o_ref, st_ref, *, tb, chunk):
    @pl.when(pl.program_id(2) == 0)
    def _():
        st_ref[...] = s0_ref[...]

    lb = lb_ref[...]
    nsub = chunk // SUB
    row_id = lax.broadcasted_iota(I32, (SUB, REC_DK), 0)
    blk_r = lax.broadcasted_iota(I32, (chunk, chunk), 0) // SUB
    blk_c = lax.broadcasted_iota(I32, (chunk, chunk), 1) // SUB

    for c in range(tb // chunk):
        rows = slice(c * chunk, (c + 1) * chunk)
        f = lb + (1.0 - lb) * jax.nn.sigmoid(rf_ref[rows, :])
        logf = jnp.log(f)
        kk = 1.0 - f
        qq = _silu(rq_ref[rows, :])
        vv = ri_ref[rows, :]
        vv_b = vv.astype(BF16)
        b = _dot_sel_lhs(tri_ref[...], logf)
        bl = _dot_sel_lhs(trib_ref[...], logf)
        tot = [bl[(i + 1) * SUB - 1:(i + 1) * SUB, :] for i in range(nsub)]
        b_end = b[chunk - 1:chunk, :]
        st = st_ref[0, 0]

        qd = qq * jnp.exp(bl)
        kend = kk * jnp.exp(jnp.concatenate([tot[i] - bl[i * SUB:(i + 1) * SUB] for i in range(nsub)], axis=0))
        kend_b = kend.astype(BF16)

        a_off = jnp.zeros((chunk, chunk), F32)
        for dist in range(nsub - 1):
            parts = []
            for i in range(nsub):
                blk = qd[i * SUB:(i + 1) * SUB]
                if dist > 0:
                    if i - dist >= 0:
                        span = tot[i - dist]
                        for r in range(i - dist + 1, i):
                            span = span + tot[r]
                        blk = blk * jnp.exp(span)
                    else:
                        blk = jnp.zeros_like(blk)
                parts.append(blk)
            qdd = jnp.concatenate(parts, axis=0).astype(BF16)
            a_off = a_off + jnp.where(blk_r - blk_c - 1 == dist, _dot_nt(qdd, kend_b), 0.0)
        o = _dot(a_off.astype(BF16), vv_b)

        o = o + _dot_nt((qq * jnp.exp(b)).astype(BF16), st.astype(BF16))

        o_diag = []
        for i in range(nsub):
            sl = slice(i * SUB, (i + 1) * SUB)
            bli, qi_, ki_, vi = bl[sl], qq[sl], kk[sl], vv[sl]
            z = []
            for t in range(SUB):
                diff = jnp.where(row_id <= t, bli[t:t + 1, :] - bli, -jnp.inf)
                z.append(qi_[t:t + 1, :] * ki_ * jnp.exp(diff))
            zsum = _dot(jnp.concatenate(z, axis=0).astype(BF16), ones_ref[...])
            for t in range(SUB):
                o_diag.append(jnp.sum(zsum[t * SUB:(t + 1) * SUB] * vi, axis=0, keepdims=True))
        o = o + jnp.concatenate(o_diag, axis=0)

        tail = [None] * nsub
        run = jnp.zeros_like(tot[0])
        for i in range(nsub - 1, -1, -1):
            tail[i] = run
            run = run + tot[i]
        kdec = kend * jnp.exp(jnp.concatenate([jnp.broadcast_to(tail[i], (SUB, REC_DK)) for i in range(nsub)], axis=0))
        st_ref[0, 0] = st * jnp.exp(b_end) + _dot(vv.T.astype(BF16), kdec.astype(BF16))

        o_ref[rows, :] = (_rms_rows(o, gn_ref[...]) * _silu(rg_ref[rows, :])).astype(o_ref.dtype)


def _hgrn(proj, lb, gn, s0t, *, tb, chunk):
    b, t, _ = proj.shape
    nh = REC_HEADS
    tri = np.tril(np.ones((chunk, chunk), np.float32))
    blk = np.arange(chunk) // SUB
    trib = tri * (blk[:, None] == blk[None, :])
    kernel = functools.partial(_hgrn_kernel, tb=tb, chunk=chunk)
    col = lambda part: pl.BlockSpec((None, tb, REC_DK), lambda bi, h, c: (bi, c, part * nh + h))
    const = lambda a: pl.BlockSpec(a.shape, lambda bi, h, c: (0,) * a.ndim)
    tri, trib = jnp.asarray(tri, BF16), jnp.asarray(trib, BF16)
    ones = jnp.ones((REC_DK, REC_DV), BF16)
    return pl.pallas_call(
        kernel,
        grid=(b, nh, t // tb),
        in_specs=[
            col(0), col(1), col(2), col(3),
            pl.BlockSpec((1, REC_DK), lambda bi, h, c: (0, h)),
            const(gn),
            pl.BlockSpec((1, 1, REC_DV, REC_DK), lambda bi, h, c: (bi, h, 0, 0)),
            const(tri), const(trib), const(ones),
        ],
        out_specs=[
            pl.BlockSpec((None, tb, REC_DV), lambda bi, h, c: (bi, c, h)),
            pl.BlockSpec((1, 1, REC_DV, REC_DK), lambda bi, h, c: (bi, h, 0, 0)),
        ],
        out_shape=[
            jax.ShapeDtypeStruct((b, t, REC_WIDTH), BF16),
            jax.ShapeDtypeStruct((b, nh, REC_DV, REC_DK), F32),
        ],
        compiler_params=_params(("arbitrary", "arbitrary", "arbitrary")),
        name="hgrn2",
    )(proj, proj, proj, proj, lb, gn, s0t, tri, trib, ones)


def _merge_kernel(x_ref, oa_ref, ob_ref, ga_ref, gb_ref, wa_ref, wb_ref, wo_ref, o_ref, m_ref):
    @pl.when(pl.program_id(1) == 0)
    def _():
        ma = jax.nn.sigmoid(ga_ref[...]) * _dot(oa_ref[...], wa_ref[...])
        mb = jax.nn.sigmoid(gb_ref[...]) * _dot(ob_ref[...], wb_ref[...])
        m_ref[...] = (ma + mb).astype(BF16)

    o_ref[...] = x_ref[...] + _dot(m_ref[...], wo_ref[...])


def _merge(x, oa, ob, gates, wa, wb, wo, tm, tn):
    n, d = x.shape
    nd = d // tn
    once = dict(pipeline_mode=pl.Buffered(1))
    return pl.pallas_call(
        _merge_kernel,
        grid=(n // tm, nd),
        in_specs=[
            pl.BlockSpec((tm, tn), lambda i, j: (i, j)),
            pl.BlockSpec((tm, oa.shape[1]), lambda i, j: (i, 0)),
            pl.BlockSpec((tm, ob.shape[1]), lambda i, j: (i, 0)),
            pl.BlockSpec((tm, d), lambda i, j: (i, 0)),
            pl.BlockSpec((tm, d), lambda i, j: (i, 1)),
            pl.BlockSpec(wa.shape, lambda i, j: (0, 0), **once),
            pl.BlockSpec(wb.shape, lambda i, j: (0, 0), **once),
            pl.BlockSpec((d, tn), lambda i, j: (0, j)),
        ],
        out_specs=pl.BlockSpec((tm, tn), lambda i, j: (i, j)),
        out_shape=jax.ShapeDtypeStruct((n, d), F32),
        scratch_shapes=[pltpu.VMEM((tm, d), BF16)],
        compiler_params=_params(("arbitrary", "arbitrary")),
        name="merge",
    )(x, oa, ob, gates, gates, wa, wb, wo)


def _pick(n, pref):
    return pref if n % pref == 0 else n


def _layer(x, past_k, past_v, past_ki, s0, lw):
    bsz, t, d = x.shape
    n = bsz * t
    past = past_k.shape[1]
    length = past + t
    topk = min(TOPK_MAX, length // 4)
    tm = _pick(n, 512)
    x2 = x.reshape(n, d)

    x1 = _ffn(x2, lw["norm_ffa"], lw["w_ffa_in"], lw["w_ffa_out"], tm, 512)

    q, k, kb, v, vb, qi, ki, kib, wi = _attn_prep(
        x1, lw["norm_mix"], lw["w_attn_in"], lw["gain_q"], lw["gain_k"], tm)
    proj_rec = _norm_proj(x1, lw["norm_mix"], lw["w_rec_in"], tm, 1024, "proj_rec")
    gates = _norm_proj(x1, lw["norm_mix"], lw["w_gate_in"], tm, _pick(2 * d, 1024), "proj_gate")

    tpad = -(-t // KEY_TILE) * KEY_TILE
    step = COUNT_TILES * KEY_TILE
    lp = -(-(past + tpad) // step) * step

    def keys(past_x, new_x, width):
        full = jnp.concatenate([past_x.reshape(bsz, past, width).astype(BF16), new_x.reshape(bsz, t, width)], axis=1)
        return jnp.pad(full, ((0, 0), (0, lp - length), (0, 0)))

    def heads(a):
        a = jnp.pad(a.reshape(bsz, t, N_HEADS, HEAD_DIM), ((0, 0), (0, tpad - t), (0, 0), (0, 0)))
        return a.transpose(0, 2, 1, 3)

    wt = jnp.pad(wi.reshape(bsz, t, IDX_HEADS), ((0, 0), (0, tpad - t), (0, 0))).transpose(0, 2, 1)
    v_all = keys(past_v, vb, KV_WIDTH).reshape(bsz, lp // KEY_TILE, KEY_TILE, N_KV_HEADS, HEAD_DIM)
    vt = jnp.concatenate(
        [v_all.transpose(0, 1, 3, 4, 2),
         jnp.ones((bsz, lp // KEY_TILE, N_KV_HEADS, V_ROWS - HEAD_DIM, KEY_TILE), BF16)], axis=3)
    o_attn = _sparse_attn(
        heads(qi), wt, heads(q), keys(past_ki, kib, IDX_DIM), keys(past_k, kb, KV_WIDTH), vt,
        _near_bias(lw["rel_bias"]), past=past, length=length, topk=topk)
    o_attn = o_attn[:, :t].reshape(n, ATTN_WIDTH)

    chunk = min(CHUNK, t)
    o_rec, st = _hgrn(proj_rec.reshape(bsz, t, 4 * REC_WIDTH), lw["lower_bound"], lw["hgrn_norm"],
                      s0.swapaxes(-1, -2), tb=_pick(t, 256), chunk=chunk)

    x2 = _merge(x1, o_attn, o_rec.reshape(n, REC_WIDTH), gates,
                lw["w_branch_attn"], lw["w_branch_rec"], lw["w_out"], tm, _pick(d, 512))
    y = _ffn(x2, lw["norm_ffb"], lw["w_ffb_in"], lw["w_ffb_out"], tm, 512)
    return (y.reshape(bsz, t, d), k.reshape(bsz, t, N_KV_HEADS, HEAD_DIM),
            v.reshape(bsz, t, N_KV_HEADS, HEAD_DIM), ki, st.swapaxes(-1, -2).astype(s0.dtype))


def _layer_weights(l, lower_bounds, norm_ffa, w_ffa_in, w_ffa_out, norm_mix, w_in, qk_gain_q, qk_gain_k,
                   rel_bias, hgrn_norm, w_branch_attn, w_branch_rec, w_out, norm_ffb, w_ffb_in, w_ffb_out):
    w = w_in[l]
    widths = (ATTN_WIDTH, KV_WIDTH, KV_WIDTH, IDX_HEADS * IDX_DIM, IDX_DIM, IDX_HEADS,
              REC_WIDTH, REC_WIDTH, REC_WIDTH, REC_WIDTH)
    splits = np.cumsum(widths)
    d = w.shape[0]
    pad = lambda a: jnp.pad(a, ((0, 0), (0, LANES - a.shape[1])))
    w_attn_in = jnp.concatenate(
        [w[:, :splits[3]], pad(w[:, splits[3]:splits[4]]), pad(w[:, splits[4]:splits[5]])], axis=1)
    row = lambda a: a.reshape(1, -1).astype(F32)
    return dict(
        norm_ffa=row(norm_ffa[l]), w_ffa_in=w_ffa_in[l].astype(BF16), w_ffa_out=w_ffa_out[l].astype(BF16),
        norm_mix=row(norm_mix[l]),
        w_attn_in=w_attn_in.astype(BF16),
        w_rec_in=w[:, splits[5]:splits[9]].astype(BF16),
        w_gate_in=w[:, splits[9]:splits[9] + 2 * d].astype(BF16),
        gain_q=row(jnp.tile(qk_gain_q[l], N_HEADS) * (HEAD_DIM ** -0.5 * LOG2E)),
        gain_k=row(jnp.tile(qk_gain_k[l], N_KV_HEADS)),
        rel_bias=rel_bias,
        lower_bound=row(lower_bounds[l]),
        hgrn_norm=row(hgrn_norm[l]),
        w_branch_attn=w_branch_attn[l].astype(BF16), w_branch_rec=w_branch_rec[l].astype(BF16),
        w_out=w_out[l].astype(BF16),
        norm_ffb=row(norm_ffb[l]), w_ffb_in=w_ffb_in[l].astype(BF16), w_ffb_out=w_ffb_out[l].astype(BF16),
    )


def _trunk(x, past_k, past_v, past_ki, s0, layers):
    ks, vs, kis, ss = [], [], [], []
    for l, lw in enumerate(layers):
        x, k, v, ki, s = _layer(x, past_k[l], past_v[l], past_ki[l], s0[l], lw)
        ks.append(k)
        vs.append(v)
        kis.append(ki.reshape(k.shape[0], k.shape[1], IDX_DIM))
        ss.append(s)
    return x, jnp.stack(ks), jnp.stack(vs), jnp.stack(kis), jnp.stack(ss)


def kernel(x_prompt, x_sample, cache_k, cache_v, cache_kidx, state_hgrn, norm_ffa, w_ffa_in, w_ffa_out,
           norm_mix, w_in, qk_gain_q, qk_gain_k, rel_bias, hgrn_lb_raw, hgrn_norm, w_branch_attn,
           w_branch_rec, w_out, norm_ffb, w_ffb_in, w_ffb_out):
    depth = w_in.shape[0]
    lb_p = jax.nn.softmax(hgrn_lb_raw.astype(F32), axis=0)
    lower_bounds = jnp.cumsum(lb_p, axis=0)[:depth]
    layers = [_layer_weights(l, lower_bounds, norm_ffa, w_ffa_in, w_ffa_out, norm_mix, w_in, qk_gain_q,
                             qk_gain_k, rel_bias, hgrn_norm, w_branch_attn, w_branch_rec, w_out,
                             norm_ffb, w_ffb_in, w_ffb_out) for l in range(depth)]
    bp = x_prompt.shape[0]
    dt = x_prompt.dtype
    empty_k = jnp.zeros((depth, bp, 0, N_KV_HEADS, HEAD_DIM), dt)
    empty_ki = jnp.zeros((depth, bp, 0, IDX_DIM), dt)
    zero_s = jnp.zeros((depth, bp, REC_HEADS, REC_DK, REC_DV), state_hgrn.dtype)
    y_p, k_p, v_p, ki_p, s_p = _trunk(x_prompt, empty_k, empty_k, empty_ki, zero_s, layers)
    y_s, k_s, v_s, ki_s, s_s = _trunk(x_sample, cache_k, cache_v, cache_kidx, state_hgrn, layers)
    return (y_p, y_s, k_p, v_p, ki_p, s_p, k_s, v_s, ki_s, s_s)
```

```python
import functools
import math

import numpy as np
import jax
import jax.numpy as jnp
from jax import lax
from jax.experimental import pallas as pl
from jax.experimental.pallas import tpu as pltpu

F32 = jnp.float32
BF16 = jnp.bfloat16
I32 = jnp.int32

EPS = 1e-6
CHUNK = 64
HEAD_DIM = 64
N_HEADS = 16
N_KV_HEADS = 4
GROUP = N_HEADS // N_KV_HEADS
IDX_HEADS = 16
IDX_DIM = 64
TOPK_MAX = 256
N_BUCKETS = 32
MAX_DISTANCE = 128
REC_HEADS = 8
REC_DK = 128
REC_DV = 128
ATTN_WIDTH = N_HEADS * HEAD_DIM
KV_WIDTH = N_KV_HEADS * HEAD_DIM
REC_WIDTH = REC_HEADS * REC_DK

LANES = 128
SUBLANES = 8
KEY_TILE = LANES
SCORE_TILES = 2
COUNT_TILES = 4
ATTN_TILES = COUNT_TILES
V_ROWS = HEAD_DIM + 16
SUB = 16
VMEM_LIMIT_BYTES = 56 * 1024 * 1024

INT_MIN = -2 ** 31
KEY_NEG_INF = int(np.array(-np.inf, np.float32).view(np.int32)) ^ 0x7FFFFFFF
MASKED = -1e30
M_INIT = -1e29
LOG2E = math.log2(math.e)


def _params(sem):
    return pltpu.CompilerParams(dimension_semantics=sem, vmem_limit_bytes=VMEM_LIMIT_BYTES)


def _dot(a, b):
    return jnp.dot(a, b, preferred_element_type=F32)


def _dot_nt(a, b):
    return lax.dot_general(a, b, (((1,), (1,)), ((), ())), preferred_element_type=F32)


def _split3(x):
    x1 = x.astype(BF16)
    r1 = x - x1.astype(F32)
    x2 = r1.astype(BF16)
    x3 = (r1 - x2.astype(F32)).astype(BF16)
    return x1, x2, x3


def _dot_sel_rhs(x, sel):
    x1, x2, x3 = _split3(x)
    return _dot(x1, sel) + _dot(x2, sel) + _dot(x3, sel)


def _dot_sel_lhs(sel, x):
    x1, x2, x3 = _split3(x)
    return _dot(sel, x1) + _dot(sel, x2) + _dot(sel, x3)


def _rms_rows(x, g):
    ms = jnp.mean(x * x, axis=-1, keepdims=True)
    return x * lax.rsqrt(ms + EPS) * g


def _silu(x):
    return x * jax.nn.sigmoid(x)


def _ffn_kernel(x_ref, g_ref, wg_ref, wu_ref, wo_ref, o_ref, h_ref):
    @pl.when(pl.program_id(1) == 0)
    def _():
        x = x_ref[...]
        h_ref[...] = _rms_rows(x, g_ref[...]).astype(BF16)
        o_ref[...] = x

    h = h_ref[...]
    gate = _dot(h, wg_ref[...])
    up = _dot(h, wu_ref[...])
    a = (_silu(gate) * up * 0.5).astype(BF16)
    o_ref[...] += _dot(a, wo_ref[...])


def _ffn(x, g, w_in, w_out, tm, tf):
    n, d = x.shape
    dff = w_out.shape[0]
    nf = dff // tf
    return pl.pallas_call(
        _ffn_kernel,
        grid=(n // tm, nf),
        in_specs=[
            pl.BlockSpec((tm, d), lambda i, j: (i, 0)),
            pl.BlockSpec((1, d), lambda i, j: (0, 0)),
            pl.BlockSpec((d, tf), lambda i, j: (0, j)),
            pl.BlockSpec((d, tf), lambda i, j: (0, j + nf)),
            pl.BlockSpec((tf, d), lambda i, j: (j, 0)),
        ],
        out_specs=pl.BlockSpec((tm, d), lambda i, j: (i, 0)),
        out_shape=jax.ShapeDtypeStruct((n, d), F32),
        scratch_shapes=[pltpu.VMEM((tm, d), BF16)],
        compiler_params=_params(("arbitrary", "arbitrary")),
        name="ffn",
    )(x, g, w_in, w_in, w_out)


def _norm_proj_kernel(x_ref, g_ref, w_ref, o_ref, h_ref):
    @pl.when(pl.program_id(1) == 0)
    def _():
        h_ref[...] = _rms_rows(x_ref[...], g_ref[...]).astype(BF16)

    o_ref[...] = _dot(h_ref[...], w_ref[...]).astype(o_ref.dtype)


def _norm_proj(x, g, w, tm, tn, name):
    n, d = x.shape
    width = w.shape[1]
    return pl.pallas_call(
        _norm_proj_kernel,
        grid=(n // tm, width // tn),
        in_specs=[
            pl.BlockSpec((tm, d), lambda i, j: (i, 0)),
            pl.BlockSpec((1, d), lambda i, j: (0, 0)),
            pl.BlockSpec((d, tn), lambda i, j: (0, j)),
        ],
        out_specs=pl.BlockSpec((tm, tn), lambda i, j: (i, j)),
        out_shape=jax.ShapeDtypeStruct((n, width), F32),
        scratch_shapes=[pltpu.VMEM((tm, d), BF16)],
        compiler_params=_params(("arbitrary", "arbitrary")),
        name=name,
    )(x, g, w)


_C_Q = 0
_C_K = _C_Q + ATTN_WIDTH
_C_V = _C_K + KV_WIDTH
_C_QI = _C_V + KV_WIDTH
_C_KI = _C_QI + IDX_HEADS * IDX_DIM
_C_WI = _C_KI + LANES
_C_END = _C_WI + LANES


def _head_rms(x, gsum_ref, gexp_ref, gain_ref):
    ss = _dot_sel_rhs(x * x, gsum_ref[...])
    r = lax.rsqrt(ss * (1.0 / HEAD_DIM) + EPS)
    return x * _dot_sel_rhs(r, gexp_ref[...]) * gain_ref[...]


def _attn_prep_kernel(x_ref, g_ref, w_ref, gq_ref, gk_ref, sq_ref, eq_ref, sk_ref, ek_ref,
                      q_ref, k_ref, kb_ref, v_ref, vb_ref, qi_ref, ki_ref, kib_ref, wi_ref):
    h = _rms_rows(x_ref[...], g_ref[...]).astype(BF16)
    q = _dot(h, w_ref[:, _C_Q:_C_K])
    q_ref[...] = _head_rms(q, sq_ref, eq_ref, gq_ref).astype(BF16)
    k = _head_rms(_dot(h, w_ref[:, _C_K:_C_V]), sk_ref, ek_ref, gk_ref)
    k_ref[...] = k
    kb_ref[...] = k.astype(BF16)
    v = _dot(h, w_ref[:, _C_V:_C_QI])
    v_ref[...] = v
    vb_ref[...] = v.astype(BF16)
    qi_ref[...] = _dot(h, w_ref[:, _C_QI:_C_KI]).astype(BF16)
    ki = _dot(h, w_ref[:, _C_KI:_C_WI])[:, :IDX_DIM]
    ki_ref[...] = ki
    kib_ref[...] = ki.astype(BF16)
    wi_ref[...] = _dot(h, w_ref[:, _C_WI:_C_END])[:, :IDX_HEADS] * (IDX_HEADS ** -0.5)


def _head_selectors(n_heads):
    width = n_heads * HEAD_DIM
    col_head = np.arange(width) // HEAD_DIM
    gsum = (col_head[:, None] == np.arange(LANES)[None, :]).astype(np.float32)
    return jnp.asarray(gsum, BF16), jnp.asarray(gsum.T, BF16)


def _attn_prep(x, g, w, gq, gk, tm):
    n, d = x.shape
    sq, eq = _head_selectors(N_HEADS)
    sk, ek = _head_selectors(N_KV_HEADS)
    row = lambda width: pl.BlockSpec((tm, width), lambda i: (i, 0))
    full = lambda a: pl.BlockSpec(a.shape, lambda i: (0,) * a.ndim, pipeline_mode=pl.Buffered(1))
    out_shapes = [
        jax.ShapeDtypeStruct((n, ATTN_WIDTH), BF16),
        jax.ShapeDtypeStruct((n, KV_WIDTH), F32),
        jax.ShapeDtypeStruct((n, KV_WIDTH), BF16),
        jax.ShapeDtypeStruct((n, KV_WIDTH), F32),
        jax.ShapeDtypeStruct((n, KV_WIDTH), BF16),
        jax.ShapeDtypeStruct((n, IDX_HEADS * IDX_DIM), BF16),
        jax.ShapeDtypeStruct((n, IDX_DIM), F32),
        jax.ShapeDtypeStruct((n, IDX_DIM), BF16),
        jax.ShapeDtypeStruct((n, IDX_HEADS), F32),
    ]
    return pl.pallas_call(
        _attn_prep_kernel,
        grid=(n // tm,),
        in_specs=[row(d), full(g), full(w), full(gq), full(gk), full(sq), full(eq), full(sk), full(ek)],
        out_specs=[row(s.shape[1]) for s in out_shapes],
        out_shape=out_shapes,
        compiler_params=_params(("arbitrary",)),
        name="attn_prep",
    )(x, g, w, gq, gk, sq, eq, sk, ek)


def _sortable(x):
    bits = lax.bitcast_convert_type(x, I32)
    return bits ^ (lax.shift_right_arithmetic(bits, 31) & 0x7FFFFFFF)


def _sparse_attn_kernel(qi_ref, wt_ref, q_ref, ki_ref, k_ref, vt_ref, bias_ref, o_ref,
                        keys_ref, stage1_ref, stage3_ref, pt_ref, acc_ref, m_ref,
                        *, past, length, topk):
    tq = KEY_TILE
    it = pl.program_id(1)
    qpos0 = past + it * tq
    kt_diag = qpos0 // KEY_TILE
    n_tiles = kt_diag + 1
    rb = 32
    mb = 64

    qi_all = qi_ref[0].reshape(IDX_HEADS * tq, IDX_DIM)
    qcol = qpos0 + lax.broadcasted_iota(I32, (rb, tq), 1)
    col_limit = jnp.minimum((lax.shift_right_logical(qcol, int(math.log2(CHUNK))) + 1) * CHUNK, length)
    krow = lax.broadcasted_iota(I32, (rb, tq), 0)

    def score_step(kt0, n_sub, masked):
        width = n_sub * KEY_TILE
        s0 = pl.multiple_of(kt0 * KEY_TILE, width)
        stage1_ref[:width, :] = _dot_nt(ki_ref[0, pl.ds(s0, width), :], qi_all)
        for r in range(width // rb):
            rows = slice(r * rb, (r + 1) * rb)
            acc = jnp.zeros((rb, tq), F32)
            for h in range(IDX_HEADS):
                d = stage1_ref[rows, h * tq:(h + 1) * tq]
                acc = acc + jnp.maximum(d, 0.0) * wt_ref[0, h:h + 1, :]
            key = _sortable(acc)
            if masked:
                key = jnp.where(s0 + r * rb + krow < col_limit, key, KEY_NEG_INF)
            keys_ref[pl.ds(s0 + r * rb, rb), :] = key

    n_steps1 = (n_tiles + SCORE_TILES - 1) // SCORE_TILES

    def far_scores(c, carry):
        score_step(SCORE_TILES * c, SCORE_TILES, False)
        return carry

    lax.fori_loop(0, n_steps1 - 1, far_scores, 0)
    score_step(SCORE_TILES * (n_steps1 - 1), SCORE_TILES, True)

    n_steps = (n_tiles + COUNT_TILES - 1) // COUNT_TILES

    def blank(kt, carry):
        keys_ref[pl.ds(pl.multiple_of(kt * KEY_TILE, KEY_TILE), KEY_TILE), :] = jnp.full(
            (KEY_TILE, tq), KEY_NEG_INF, I32)
        return carry

    lax.fori_loop(n_steps1 * SCORE_TILES, n_steps * COUNT_TILES, blank, 0)

    step_keys = COUNT_TILES * KEY_TILE

    def bit_step(state):
        bi, thr, still_open, _ = state
        cand = thr ^ lax.shift_left(jnp.int32(1), 31 - bi)

        def count(j, cnt):
            blk = keys_ref[pl.ds(pl.multiple_of(j * step_keys, step_keys), step_keys), :]
            ones = jnp.where(blk >= cand, 1, 0)
            return cnt + jnp.sum(ones.reshape(step_keys // SUBLANES, SUBLANES, tq), axis=0)

        cnt = lax.fori_loop(0, n_steps, count, jnp.zeros((SUBLANES, tq), I32))
        total = jnp.sum(cnt, axis=0, keepdims=True)
        keep = total >= topk
        still_open = jnp.where(keep, jnp.where(total != topk, 1, 0), still_open)
        return bi + 1, jnp.where(keep, cand, thr), still_open, jnp.max(still_open)

    _, thr, _, _ = lax.while_loop(
        lambda s: (s[0] < 32) & (s[3] > 0), bit_step,
        (jnp.int32(0), jnp.full((1, tq), INT_MIN, I32), jnp.ones((1, tq), I32), jnp.int32(1)))
    thr = jnp.maximum(thr, KEY_NEG_INF + 1)

    def to_mask(j, carry):
        sl = pl.ds(pl.multiple_of(j * step_keys, step_keys), step_keys)
        mask = jnp.where(keys_ref[sl, :] >= thr, 0.0, MASKED)
        keys_ref[sl, :] = lax.bitcast_convert_type(mask, I32)
        return carry

    lax.fori_loop(0, n_steps, to_mask, 0)

    cols4 = GROUP * tq
    width = ATTN_TILES * KEY_TILE
    acc_ref[...] = jnp.zeros(acc_ref.shape, F32)
    m_ref[...] = jnp.full(m_ref.shape, M_INIT, F32)

    def logits(c, g):
        s0 = pl.multiple_of(c * width, width)
        q4 = q_ref[0, g * GROUP:(g + 1) * GROUP].reshape(cols4, HEAD_DIM)
        kg = k_ref[0, pl.ds(s0, width), :][:, g * HEAD_DIM:(g + 1) * HEAD_DIM]
        stage3_ref[g] = _dot_nt(kg, q4)

    def softmax(c, g, with_bias):
        s0 = pl.multiple_of(c * width, width)
        cmax = [None] * GROUP
        for r in range(width // mb):
            rows = slice(r * mb, (r + 1) * mb)
            mask = lax.bitcast_convert_type(keys_ref[pl.ds(s0 + r * mb, mb), :], F32)
            if with_bias:
                u = (r * mb) // KEY_TILE
                bias_tile = jnp.clip(c * ATTN_TILES + u - (kt_diag - 2), 0, 2)
                brows = slice((r * mb) % KEY_TILE, (r * mb) % KEY_TILE + mb)
            for hh in range(GROUP):
                cols = slice(hh * tq, (hh + 1) * tq)
                s = stage3_ref[g, rows, cols] + mask
                if with_bias:
                    s = s + bias_ref[g * GROUP + hh, bias_tile, brows, :]
                stage3_ref[g, rows, cols] = s
                part = jnp.max(s.reshape(mb // SUBLANES, SUBLANES, tq), axis=0)
                cmax[hh] = part if r == 0 else jnp.maximum(cmax[hh], part)
        cmax = [jnp.max(cm, axis=0, keepdims=True) for cm in cmax]
        m_old = m_ref[g]
        m_new = jnp.maximum(m_old, jnp.concatenate(cmax, axis=1))
        m_ref[g] = m_new
        for r in range(width // rb):
            rows = slice(r * rb, (r + 1) * rb)
            pt_ref[g, rows, :] = jnp.exp2(stage3_ref[g, rows, :] - m_new).astype(BF16)
        return jnp.exp2(m_old - m_new)

    def values(c, g, alpha):
        vt = jnp.concatenate([vt_ref[0, c * ATTN_TILES + u, g] for u in range(ATTN_TILES)], axis=1)
        acc_ref[g] = acc_ref[g] * alpha + _dot(vt, pt_ref[g])

    g_last = N_KV_HEADS - 1

    def attend_steps(c_lo, c_hi, with_bias):
        last = jnp.maximum(c_hi - 1, 0)
        logits(jnp.minimum(c_lo, last), 0)
        pt_ref[g_last] = jnp.zeros(pt_ref.shape[1:], BF16)

        def body(c, alpha_prev):
            for g in range(N_KV_HEADS):
                if g < g_last:
                    logits(c, g + 1)
                else:
                    logits(jnp.minimum(c + 1, last), 0)
                alpha = softmax(c, g, with_bias)
                if g == 0:
                    values(jnp.maximum(c - 1, 0), g_last, alpha_prev)
                else:
                    values(c, g - 1, alpha_prev)
                alpha_prev = alpha
            return alpha_prev

        alpha_last = lax.fori_loop(c_lo, c_hi, body, jnp.ones((1, cols4), F32))
        values(last, g_last, alpha_last)

    n_far3 = jnp.maximum(n_steps - 2, 0)
    attend_steps(0, n_far3, False)
    attend_steps(n_far3, n_steps, True)

    for g in range(N_KV_HEADS):
        acc = acc_ref[g]
        out_t = acc[:HEAD_DIM] / acc[HEAD_DIM:HEAD_DIM + 1]
        for pair in range(GROUP // 2):
            two = jnp.concatenate([out_t[:, (2 * pair) * tq:(2 * pair + 1) * tq],
                                   out_t[:, (2 * pair + 1) * tq:(2 * pair + 2) * tq]], axis=0)
            c0 = (g * GROUP + 2 * pair) * HEAD_DIM
            o_ref[0, :, c0:c0 + 2 * HEAD_DIM] = two.T.astype(o_ref.dtype)


def _t5_bucket(rel):
    half = N_BUCKETS // 2
    max_exact = half // 2
    n = jnp.abs(rel)
    large = max_exact + (jnp.log(jnp.maximum(n, max_exact).astype(jnp.float32) / max_exact)
                         / math.log(MAX_DISTANCE / max_exact) * (half - max_exact)).astype(jnp.int32)
    large = jnp.minimum(large, half - 1)
    return jnp.where(rel > 0, half, 0) + jnp.where(n < max_exact, n, large)


def _near_bias(rel_bias):
    t = jnp.arange(KEY_TILE, dtype=I32)[None, :]
    s = jnp.arange(3 * KEY_TILE, dtype=I32)[:, None] - 2 * KEY_TILE
    bias = rel_bias[_t5_bucket(s - t)].astype(F32) - rel_bias[N_BUCKETS // 2 - 1].astype(F32)
    return (bias * LOG2E).reshape(3, KEY_TILE, KEY_TILE, N_HEADS).transpose(3, 0, 1, 2)


def _sparse_attn(qi, wt, q, ki, k, vt, bias, *, past, length, topk):
    b, _, t, _ = q.shape
    lp = k.shape[1]
    tq = KEY_TILE
    assert past % KEY_TILE == 0 and t % tq == 0 and lp % (COUNT_TILES * KEY_TILE) == 0
    assert lp >= -(-(past + t) // (COUNT_TILES * KEY_TILE)) * COUNT_TILES * KEY_TILE
    kernel = functools.partial(_sparse_attn_kernel, past=past, length=length, topk=topk)
    once = dict(pipeline_mode=pl.Buffered(1))
    return pl.pallas_call(
        kernel,
        grid=(b, t // tq),
        in_specs=[
            pl.BlockSpec((1, IDX_HEADS, tq, IDX_DIM), lambda bi, i: (bi, 0, i, 0)),
            pl.BlockSpec((1, IDX_HEADS, tq), lambda bi, i: (bi, 0, i)),
            pl.BlockSpec((1, N_HEADS, tq, HEAD_DIM), lambda bi, i: (bi, 0, i, 0)),
            pl.BlockSpec((1, lp, IDX_DIM), lambda bi, i: (bi, 0, 0), **once),
            pl.BlockSpec((1, lp, KV_WIDTH), lambda bi, i: (bi, 0, 0), **once),
            pl.BlockSpec((1, lp // KEY_TILE, N_KV_HEADS, V_ROWS, KEY_TILE), lambda bi, i: (bi, 0, 0, 0, 0), **once),
            pl.BlockSpec(bias.shape, lambda bi, i: (0, 0, 0, 0), **once),
        ],
        out_specs=pl.BlockSpec((1, tq, ATTN_WIDTH), lambda bi, i: (bi, i, 0)),
        out_shape=jax.ShapeDtypeStruct((b, t, ATTN_WIDTH), BF16),
        scratch_shapes=[
            pltpu.VMEM((lp, tq), I32),
            pltpu.VMEM((SCORE_TILES * KEY_TILE, IDX_HEADS * tq), F32),
            pltpu.VMEM((N_KV_HEADS, ATTN_TILES * KEY_TILE, GROUP * tq), F32),
            pltpu.VMEM((N_KV_HEADS, ATTN_TILES * KEY_TILE, GROUP * tq), BF16),
            pltpu.VMEM((N_KV_HEADS, V_ROWS, GROUP * tq), F32),
            pltpu.VMEM((N_KV_HEADS, 1, GROUP * tq), F32),
        ],
        compiler_params=_params(("arbitrary", "arbitrary")),
        name="sparse_attn",
    )(qi, wt, q, ki, k, vt, bias)


def _hgrn_kernel(rf_ref, rq_ref, ri_ref, rg_ref, lb_ref, gn_ref, s0_ref, tri_ref, trib_ref, ones_ref,
                 o_ref, st_ref, *, tb, chunk):
    @pl.when(pl.program_id(2) == 0)
    def _():
        st_ref[...] = s0_ref[...]

    lb = lb_ref[...]
    nsub = chunk // SUB
    row_id = lax.broadcasted_iota(I32, (SUB, REC_DK), 0)
    blk_r = lax.broadcasted_iota(I32, (chunk, chunk), 0) // SUB
    blk_c = lax.broadcasted_iota(I32, (chunk, chunk), 1) // SUB

    for c in range(tb // chunk):
        rows = slice(c * chunk, (c + 1) * chunk)
        f = lb + (1.0 - lb) * jax.nn.sigmoid(rf_ref[rows, :])
        logf = jnp.log(f)
        kk = 1.0 - f
        qq = _silu(rq_ref[rows, :])
        vv = ri_ref[rows, :]
        vv_b = vv.astype(BF16)
        b = _dot_sel_lhs(tri_ref[...], logf)
        bl = _dot_sel_lhs(trib_ref[...], logf)
        tot = [bl[(i + 1) * SUB - 1:(i + 1) * SUB, :] for i in range(nsub)]
        b_end = b[chunk - 1:chunk, :]
        st = st_ref[0, 0]

        qd = qq * jnp.exp(bl)
        kend = kk * jnp.exp(jnp.concatenate([tot[i] - bl[i * SUB:(i + 1) * SUB] for i in range(nsub)], axis=0))
        kend_b = kend.astype(BF16)

        a_off = jnp.zeros((chunk, chunk), F32)
        for dist in range(nsub - 1):
            parts = []
            for i in range(nsub):
                blk = qd[i * SUB:(i + 1) * SUB]
                if dist > 0:
                    if i - dist >= 0:
                        span = tot[i - dist]
                        for r in range(i - dist + 1, i):
                            span = span + tot[r]
                        blk = blk * jnp.exp(span)
                    else:
                        blk = jnp.zeros_like(blk)
                parts.append(blk)
            qdd = jnp.concatenate(parts, axis=0).astype(BF16)
            a_off = a_off + jnp.where(blk_r - blk_c - 1 == dist, _dot_nt(qdd, kend_b), 0.0)
        o = _dot(a_off.astype(BF16), vv_b)

        o = o + _dot_nt((qq * jnp.exp(b)).astype(BF16), st.astype(BF16))

        o_diag = []
        for i in range(nsub):
            sl = slice(i * SUB, (i + 1) * SUB)
            bli, qi_, ki_, vi = bl[sl], qq[sl], kk[sl], vv[sl]
            z = []
            for t in range(SUB):
                diff = jnp.where(row_id <= t, bli[t:t + 1, :] - bli, -jnp.inf)
                z.append(qi_[t:t + 1, :] * ki_ * jnp.exp(diff))
            zsum = _dot(jnp.concatenate(z, axis=0).astype(BF16), ones_ref[...])
            for t in range(SUB):
                o_diag.append(jnp.sum(zsum[t * SUB:(t + 1) * SUB] * vi, axis=0, keepdims=True))
        o = o + jnp.concatenate(o_diag, axis=0)

        tail = [None] * nsub
        run = jnp.zeros_like(tot[0])
        for i in range(nsub - 1, -1, -1):
            tail[i] = run
            run = run + tot[i]
        kdec = kend * jnp.exp(jnp.concatenate([jnp.broadcast_to(tail[i], (SUB, REC_DK)) for i in range(nsub)], axis=0))
        st_ref[0, 0] = st * jnp.exp(b_end) + _dot(vv.T.astype(BF16), kdec.astype(BF16))

        o_ref[rows, :] = (_rms_rows(o, gn_ref[...]) * _silu(rg_ref[rows, :])).astype(o_ref.dtype)


def _hgrn(proj, lb, gn, s0t, *, tb, chunk):
    b, t, _ = proj.shape
    nh = REC_HEADS
    tri = np.tril(np.ones((chunk, chunk), np.float32))
    blk = np.arange(chunk) // SUB
    trib = tri * (blk[:, None] == blk[None, :])
    kernel = functools.partial(_hgrn_kernel, tb=tb, chunk=chunk)
    col = lambda part: pl.BlockSpec((None, tb, REC_DK), lambda bi, h, c: (bi, c, part * nh + h))
    const = lambda a: pl.BlockSpec(a.shape, lambda bi, h, c: (0,) * a.ndim)
    tri, trib = jnp.asarray(tri, BF16), jnp.asarray(trib, BF16)
    ones = jnp.ones((REC_DK, REC_DV), BF16)
    return pl.pallas_call(
        kernel,
        grid=(b, nh, t // tb),
        in_specs=[
            col(0), col(1), col(2), col(3),
            pl.BlockSpec((1, REC_DK), lambda bi, h, c: (0, h)),
            const(gn),
            pl.BlockSpec((1, 1, REC_DV, REC_DK), lambda bi, h, c: (bi, h, 0, 0)),
            const(tri), const(trib), const(ones),
        ],
        out_specs=[
            pl.BlockSpec((None, tb, REC_DV), lambda bi, h, c: (bi, c, h)),
            pl.BlockSpec((1, 1, REC_DV, REC_DK), lambda bi, h, c: (bi, h, 0, 0)),
        ],
        out_shape=[
            jax.ShapeDtypeStruct((b, t, REC_WIDTH), BF16),
            jax.ShapeDtypeStruct((b, nh, REC_DV, REC_DK), F32),
        ],
        compiler_params=_params(("arbitrary", "arbitrary", "arbitrary")),
        name="hgrn2",
    )(proj, proj, proj, proj, lb, gn, s0t, tri, trib, ones)


def _merge_kernel(x_ref, oa_ref, ob_ref, ga_ref, gb_ref, wa_ref, wb_ref, wo_ref, o_ref, m_ref):
    @pl.when(pl.program_id(1) == 0)
    def _():
        ma = jax.nn.sigmoid(ga_ref[...]) * _dot(oa_ref[...], wa_ref[...])
        mb = jax.nn.sigmoid(gb_ref[...]) * _dot(ob_ref[...], wb_ref[...])
        m_ref[...] = (ma + mb).astype(BF16)

    o_ref[...] = x_ref[...] + _dot(m_ref[...], wo_ref[...])


def _merge(x, oa, ob, gates, wa, wb, wo, tm, tn):
    n, d = x.shape
    nd = d // tn
    once = dict(pipeline_mode=pl.Buffered(1))
    return pl.pallas_call(
        _merge_kernel,
        grid=(n // tm, nd),
        in_specs=[
            pl.BlockSpec((tm, tn), lambda i, j: (i, j)),
            pl.BlockSpec((tm, oa.shape[1]), lambda i, j: (i, 0)),
            pl.BlockSpec((tm, ob.shape[1]), lambda i, j: (i, 0)),
            pl.BlockSpec((tm, d), lambda i, j: (i, 0)),
            pl.BlockSpec((tm, d), lambda i, j: (i, 1)),
            pl.BlockSpec(wa.shape, lambda i, j: (0, 0), **once),
            pl.BlockSpec(wb.shape, lambda i, j: (0, 0), **once),
            pl.BlockSpec((d, tn), lambda i, j: (0, j)),
        ],
        out_specs=pl.BlockSpec((tm, tn), lambda i, j: (i, j)),
        out_shape=jax.ShapeDtypeStruct((n, d), F32),
        scratch_shapes=[pltpu.VMEM((tm, d), BF16)],
        compiler_params=_params(("arbitrary", "arbitrary")),
        name="merge",
    )(x, oa, ob, gates, gates, wa, wb, wo)


def _pick(n, pref):
    return pref if n % pref == 0 else n


def _layer(x, past_k, past_v, past_ki, s0, lw):
    bsz, t, d = x.shape
    n = bsz * t
    past = past_k.shape[1]
    length = past + t
    topk = min(TOPK_MAX, length // 4)
    tm = _pick(n, 512)
    x2 = x.reshape(n, d)

    x1 = _ffn(x2, lw["norm_ffa"], lw["w_ffa_in"], lw["w_ffa_out"], tm, 512)

    q, k, kb, v, vb, qi, ki, kib, wi = _attn_prep(
        x1, lw["norm_mix"], lw["w_attn_in"], lw["gain_q"], lw["gain_k"], tm)
    proj_rec = _norm_proj(x1, lw["norm_mix"], lw["w_rec_in"], tm, 1024, "proj_rec")
    gates = _norm_proj(x1, lw["norm_mix"], lw["w_gate_in"], tm, _pick(2 * d, 1024), "proj_gate")

    tpad = -(-t // KEY_TILE) * KEY_TILE
    step = COUNT_TILES * KEY_TILE
    lp = -(-(past + tpad) // step) * step

    def keys(past_x, new_x, width):
        full = jnp.concatenate([past_x.reshape(bsz, past, width).astype(BF16), new_x.reshape(bsz, t, width)], axis=1)
        return jnp.pad(full, ((0, 0), (0, lp - length), (0, 0)))

    def heads(a):
        a = jnp.pad(a.reshape(bsz, t, N_HEADS, HEAD_DIM), ((0, 0), (0, tpad - t), (0, 0), (0, 0)))
        return a.transpose(0, 2, 1, 3)

    wt = jnp.pad(wi.reshape(bsz, t, IDX_HEADS), ((0, 0), (0, tpad - t), (0, 0))).transpose(0, 2, 1)
    v_all = keys(past_v, vb, KV_WIDTH).reshape(bsz, lp // KEY_TILE, KEY_TILE, N_KV_HEADS, HEAD_DIM)
    vt = jnp.concatenate(
        [v_all.transpose(0, 1, 3, 4, 2),
         jnp.ones((bsz, lp // KEY_TILE, N_KV_HEADS, V_ROWS - HEAD_DIM, KEY_TILE), BF16)], axis=3)
    o_attn = _sparse_attn(
        heads(qi), wt, heads(q), keys(past_ki, kib, IDX_DIM), keys(past_k, kb, KV_WIDTH), vt,
        _near_bias(lw["rel_bias"]), past=past, length=length, topk=topk)
    o_attn = o_attn[:, :t].reshape(n, ATTN_WIDTH)

    chunk = min(CHUNK, t)
    o_rec, st = _hgrn(proj_rec.reshape(bsz, t, 4 * REC_WIDTH), lw["lower_bound"], lw["hgrn_norm"],
                      s0.swapaxes(-1, -2), tb=_pick(t, 256), chunk=chunk)

    x2 = _merge(x1, o_attn, o_rec.reshape(n, REC_WIDTH), gates,
                lw["w_branch_attn"], lw["w_branch_rec"], lw["w_out"], tm, _pick(d, 512))
    y = _ffn(x2, lw["norm_ffb"], lw["w_ffb_in"], lw["w_ffb_out"], tm, 512)
    return (y.reshape(bsz, t, d), k.reshape(bsz, t, N_KV_HEADS, HEAD_DIM),
            v.reshape(bsz, t, N_KV_HEADS, HEAD_DIM), ki, st.swapaxes(-1, -2).astype(s0.dtype))


def _layer_weights(l, lower_bounds, norm_ffa, w_ffa_in, w_ffa_out, norm_mix, w_in, qk_gain_q, qk_gain_k,
                   rel_bias, hgrn_norm, w_branch_attn, w_branch_rec, w_out, norm_ffb, w_ffb_in, w_ffb_out):
    w = w_in[l]
    widths = (ATTN_WIDTH, KV_WIDTH, KV_WIDTH, IDX_HEADS * IDX_DIM, IDX_DIM, IDX_HEADS,
              REC_WIDTH, REC_WIDTH, REC_WIDTH, REC_WIDTH)
    splits = np.cumsum(widths)
    d = w.shape[0]
    pad = lambda a: jnp.pad(a, ((0, 0), (0, LANES - a.shape[1])))
    w_attn_in = jnp.concatenate(
        [w[:, :splits[3]], pad(w[:, splits[3]:splits[4]]), pad(w[:, splits[4]:splits[5]])], axis=1)
    row = lambda a: a.reshape(1, -1).astype(F32)
    return dict(
        norm_ffa=row(norm_ffa[l]), w_ffa_in=w_ffa_in[l].astype(BF16), w_ffa_out=w_ffa_out[l].astype(BF16),
        norm_mix=row(norm_mix[l]),
        w_attn_in=w_attn_in.astype(BF16),
        w_rec_in=w[:, splits[5]:splits[9]].astype(BF16),
        w_gate_in=w[:, splits[9]:splits[9] + 2 * d].astype(BF16),
        gain_q=row(jnp.tile(qk_gain_q[l], N_HEADS) * (HEAD_DIM ** -0.5 * LOG2E)),
        gain_k=row(jnp.tile(qk_gain_k[l], N_KV_HEADS)),
        rel_bias=rel_bias,
        lower_bound=row(lower_bounds[l]),
        hgrn_norm=row(hgrn_norm[l]),
        w_branch_attn=w_branch_attn[l].astype(BF16), w_branch_rec=w_branch_rec[l].astype(BF16),
        w_out=w_out[l].astype(BF16),
        norm_ffb=row(norm_ffb[l]), w_ffb_in=w_ffb_in[l].astype(BF16), w_ffb_out=w_ffb_out[l].astype(BF16),
    )


def _trunk(x, past_k, past_v, past_ki, s0, layers):
    ks, vs, kis, ss = [], [], [], []
    for l, lw in enumerate(layers):
        x, k, v, ki, s = _layer(x, past_k[l], past_v[l], past_ki[l], s0[l], lw)
        ks.append(k)
        vs.append(v)
        kis.append(ki.reshape(k.shape[0], k.shape[1], IDX_DIM))
        ss.append(s)
    return x, jnp.stack(ks), jnp.stack(vs), jnp.stack(kis), jnp.stack(ss)


def kernel(x_prompt, x_sample, cache_k, cache_v, cache_kidx, state_hgrn, norm_ffa, w_ffa_in, w_ffa_out,
           norm_mix, w_in, qk_gain_q, qk_gain_k, rel_bias, hgrn_lb_raw, hgrn_norm, w_branch_attn,
           w_branch_rec, w_out, norm_ffb, w_ffb_in, w_ffb_out):
    depth = w_in.shape[0]
    lb_p = jax.nn.softmax(hgrn_lb_raw.astype(F32), axis=0)
    lower_bounds = jnp.cumsum(lb_p, axis=0)[:depth]
    layers = [_layer_weights(l, lower_bounds, norm_ffa, w_ffa_in, w_ffa_out, norm_mix, w_in, qk_gain_q,
                             qk_gain_k, rel_bias, hgrn_norm, w_branch_attn, w_branch_rec, w_out,
                             norm_ffb, w_ffb_in, w_ffb_out) for l in range(depth)]
    bp = x_prompt.shape[0]
    dt = x_prompt.dtype
    empty_k = jnp.zeros((depth, bp, 0, N_KV_HEADS, HEAD_DIM), dt)
    empty_ki = jnp.zeros((depth, bp, 0, IDX_DIM), dt)
    zero_s = jnp.zeros((depth, bp, REC_HEADS, REC_DK, REC_DV), state_hgrn.dtype)
    y_p, k_p, v_p, ki_p, s_p = _trunk(x_prompt, empty_k, empty_k, empty_ki, zero_s, layers)
    y_s, k_s, v_s, ki_s, s_s = _trunk(x_sample, cache_k, cache_v, cache_kidx, state_hgrn, layers)
    return (y_p, y_s, k_p, v_p, ki_p, s_p, k_s, v_s, ki_s, s_s)
```

```python
import functools
import math

import numpy as np
import jax
import jax.numpy as jnp
from jax import lax
from jax.experimental import pallas as pl
from jax.experimental.pallas import tpu as pltpu

F32 = jnp.float32
BF16 = jnp.bfloat16
I32 = jnp.int32

EPS = 1e-6
CHUNK = 64
HEAD_DIM = 64
N_HEADS = 16
N_KV_HEADS = 4
GROUP = N_HEADS // N_KV_HEADS
IDX_HEADS = 16
IDX_DIM = 64
TOPK_MAX = 256
N_BUCKETS = 32
MAX_DISTANCE = 128
REC_HEADS = 8
REC_DK = 128
REC_DV = 128
ATTN_WIDTH = N_HEADS * HEAD_DIM
KV_WIDTH = N_KV_HEADS * HEAD_DIM
REC_WIDTH = REC_HEADS * REC_DK

LANES = 128
SUBLANES = 8
KEY_TILE = LANES
SCORE_TILES = 2
COUNT_TILES = 4
ATTN_TILES = COUNT_TILES
V_ROWS = HEAD_DIM + 16
SUB = 16
VMEM_LIMIT_BYTES = 56 * 1024 * 1024

INT_MIN = -2 ** 31
KEY_NEG_INF = int(np.array(-np.inf, np.float32).view(np.int32)) ^ 0x7FFFFFFF
MASKED = -1e30
M_INIT = -1e29
LOG2E = math.log2(math.e)


def _params(sem):
    return pltpu.CompilerParams(dimension_semantics=sem, vmem_limit_bytes=VMEM_LIMIT_BYTES)


def _dot(a, b):
    return jnp.dot(a, b, preferred_element_type=F32)


def _dot_nt(a, b):
    return lax.dot_general(a, b, (((1,), (1,)), ((), ())), preferred_element_type=F32)


def _split3(x):
    x1 = x.astype(BF16)
    r1 = x - x1.astype(F32)
    x2 = r1.astype(BF16)
    x3 = (r1 - x2.astype(F32)).astype(BF16)
    return x1, x2, x3


def _dot_sel_rhs(x, sel):
    x1, x2, x3 = _split3(x)
    return _dot(x1, sel) + _dot(x2, sel) + _dot(x3, sel)


def _dot_sel_lhs(sel, x):
    x1, x2, x3 = _split3(x)
    return _dot(sel, x1) + _dot(sel, x2) + _dot(sel, x3)


def _rms_rows(x, g):
    ms = jnp.mean(x * x, axis=-1, keepdims=True)
    return x * lax.rsqrt(ms + EPS) * g


def _silu(x):
    return x * jax.nn.sigmoid(x)


def _ffn_kernel(x_ref, g_ref, wg_ref, wu_ref, wo_ref, o_ref, h_ref):
    @pl.when(pl.program_id(1) == 0)
    def _():
        x = x_ref[...]
        h_ref[...] = _rms_rows(x, g_ref[...]).astype(BF16)
        o_ref[...] = x

    h = h_ref[...]
    gate = _dot(h, wg_ref[...])
    up = _dot(h, wu_ref[...])
    a = (_silu(gate) * up * 0.5).astype(BF16)
    o_ref[...] += _dot(a, wo_ref[...])


def _ffn(x, g, w_in, w_out, tm, tf):
    n, d = x.shape
    dff = w_out.shape[0]
    nf = dff // tf
    return pl.pallas_call(
        _ffn_kernel,
        grid=(n // tm, nf),
        in_specs=[
            pl.BlockSpec((tm, d), lambda i, j: (i, 0)),
            pl.BlockSpec((1, d), lambda i, j: (0, 0)),
            pl.BlockSpec((d, tf), lambda i, j: (0, j)),
            pl.BlockSpec((d, tf), lambda i, j: (0, j + nf)),
            pl.BlockSpec((tf, d), lambda i, j: (j, 0)),
        ],
        out_specs=pl.BlockSpec((tm, d), lambda i, j: (i, 0)),
        out_shape=jax.ShapeDtypeStruct((n, d), F32),
        scratch_shapes=[pltpu.VMEM((tm, d), BF16)],
        compiler_params=_params(("arbitrary", "arbitrary")),
        name="ffn",
    )(x, g, w_in, w_in, w_out)


def _norm_proj_kernel(x_ref, g_ref, w_ref, o_ref, h_ref):
    @pl.when(pl.program_id(1) == 0)
    def _():
        h_ref[...] = _rms_rows(x_ref[...], g_ref[...]).astype(BF16)

    o_ref[...] = _dot(h_ref[...], w_ref[...]).astype(o_ref.dtype)


def _norm_proj(x, g, w, tm, tn, name):
    n, d = x.shape
    width = w.shape[1]
    return pl.pallas_call(
        _norm_proj_kernel,
        grid=(n // tm, width // tn),
        in_specs=[
            pl.BlockSpec((tm, d), lambda i, j: (i, 0)),
            pl.BlockSpec((1, d), lambda i, j: (0, 0)),
            pl.BlockSpec((d, tn), lambda i, j: (0, j)),
        ],
        out_specs=pl.BlockSpec((tm, tn), lambda i, j: (i, j)),
        out_shape=jax.ShapeDtypeStruct((n, width), F32),
        scratch_shapes=[pltpu.VMEM((tm, d), BF16)],
        compiler_params=_params(("arbitrary", "arbitrary")),
        name=name,
    )(x, g, w)


_C_Q = 0
_C_K = _C_Q + ATTN_WIDTH
_C_V = _C_K + KV_WIDTH
_C_QI = _C_V + KV_WIDTH
_C_KI = _C_QI + IDX_HEADS * IDX_DIM
_C_WI = _C_KI + LANES
_C_END = _C_WI + LANES


def _head_rms(x, gsum_ref, gexp_ref, gain_ref):
    ss = _dot_sel_rhs(x * x, gsum_ref[...])
    r = lax.rsqrt(ss * (1.0 / HEAD_DIM) + EPS)
    return x * _dot_sel_rhs(r, gexp_ref[...]) * gain_ref[...]


def _attn_prep_kernel(x_ref, g_ref, w_ref, gq_ref, gk_ref, sq_ref, eq_ref, sk_ref, ek_ref,
                      q_ref, k_ref, kb_ref, v_ref, vb_ref, qi_ref, ki_ref, kib_ref, wi_ref):
    h = _rms_rows(x_ref[...], g_ref[...]).astype(BF16)
    q = _dot(h, w_ref[:, _C_Q:_C_K])
    q_ref[...] = _head_rms(q, sq_ref, eq_ref, gq_ref).astype(BF16)
    k = _head_rms(_dot(h, w_ref[:, _C_K:_C_V]), sk_ref, ek_ref, gk_ref)
    k_ref[...] = k
    kb_ref[...] = k.astype(BF16)
    v = _dot(h, w_ref[:, _C_V:_C_QI])
    v_ref[...] = v
    vb_ref[...] = v.astype(BF16)
    qi_ref[...] = _dot(h, w_ref[:, _C_QI:_C_KI]).astype(BF16)
    ki = _dot(h, w_ref[:, _C_KI:_C_WI])[:, :IDX_DIM]
    ki_ref[...] = ki
    kib_ref[...] = ki.astype(BF16)
    wi_ref[...] = _dot(h, w_ref[:, _C_WI:_C_END])[:, :IDX_HEADS] * (IDX_HEADS ** -0.5)


def _head_selectors(n_heads):
    width = n_heads * HEAD_DIM
    col_head = np.arange(width) // HEAD_DIM
    gsum = (col_head[:, None] == np.arange(LANES)[None, :]).astype(np.float32)
    return jnp.asarray(gsum, BF16), jnp.asarray(gsum.T, BF16)


def _attn_prep(x, g, w, gq, gk, tm):
    n, d = x.shape
    sq, eq = _head_selectors(N_HEADS)
    sk, ek = _head_selectors(N_KV_HEADS)
    row = lambda width: pl.BlockSpec((tm, width), lambda i: (i, 0))
    full = lambda a: pl.BlockSpec(a.shape, lambda i: (0,) * a.ndim, pipeline_mode=pl.Buffered(1))
    out_shapes = [
        jax.ShapeDtypeStruct((n, ATTN_WIDTH), BF16),
        jax.ShapeDtypeStruct((n, KV_WIDTH), F32),
        jax.ShapeDtypeStruct((n, KV_WIDTH), BF16),
        jax.ShapeDtypeStruct((n, KV_WIDTH), F32),
        jax.ShapeDtypeStruct((n, KV_WIDTH), BF16),
        jax.ShapeDtypeStruct((n, IDX_HEADS * IDX_DIM), BF16),
        jax.ShapeDtypeStruct((n, IDX_DIM), F32),
        jax.ShapeDtypeStruct((n, IDX_DIM), BF16),
        jax.ShapeDtypeStruct((n, IDX_HEADS), F32),
    ]
    return pl.pallas_call(
        _attn_prep_kernel,
        grid=(n // tm,),
        in_specs=[row(d), full(g), full(w), full(gq), full(gk), full(sq), full(eq), full(sk), full(ek)],
        out_specs=[row(s.shape[1]) for s in out_shapes],
        out_shape=out_shapes,
        compiler_params=_params(("arbitrary",)),
        name="attn_prep",
    )(x, g, w, gq, gk, sq, eq, sk, ek)


def _sortable(x):
    bits = lax.bitcast_convert_type(x, I32)
    return bits ^ (lax.shift_right_arithmetic(bits, 31) & 0x7FFFFFFF)


def _sparse_attn_kernel(qi_ref, wt_ref, q_ref, ki_ref, k_ref, vt_ref, bias_ref, o_ref,
                        keys_ref, stage1_ref, stage3_ref, pt_ref, acc_ref, m_ref,
                        *, past, length, topk):
    tq = KEY_TILE
    it = pl.program_id(1)
    qpos0 = past + it * tq
    kt_diag = qpos0 // KEY_TILE
    n_tiles = kt_diag + 1
    rb = 32
    mb = 64

    qi_all = qi_ref[0].reshape(IDX_HEADS * tq, IDX_DIM)
    qcol = qpos0 + lax.broadcasted_iota(I32, (rb, tq), 1)
    col_limit = jnp.minimum((lax.shift_right_logical(qcol, int(math.log2(CHUNK))) + 1) * CHUNK, length)
    krow = lax.broadcasted_iota(I32, (rb, tq), 0)

    def score_step(kt0, n_sub, masked):
        width = n_sub * KEY_TILE
        s0 = pl.multiple_of(kt0 * KEY_TILE, width)
        stage1_ref[:width, :] = _dot_nt(ki_ref[0, pl.ds(s0, width), :], qi_all)
        for r in range(width // rb):
            rows = slice(r * rb, (r + 1) * rb)
            acc = jnp.zeros((rb, tq), F32)
            for h in range(IDX_HEADS):
                d = stage1_ref[rows, h * tq:(h + 1) * tq]
                acc = acc + jnp.maximum(d, 0.0) * wt_ref[0, h:h + 1, :]
            key = _sortable(acc)
            if masked:
                key = jnp.where(s0 + r * rb + krow < col_limit, key, KEY_NEG_INF)
            keys_ref[pl.ds(s0 + r * rb, rb), :] = key

    n_steps1 = (n_tiles + SCORE_TILES - 1) // SCORE_TILES

    def far_scores(c, carry):
        score_step(SCORE_TILES * c, SCORE_TILES, False)
        return carry

    lax.fori_loop(0, n_steps1 - 1, far_scores, 0)
    score_step(SCORE_TILES * (n_steps1 - 1), SCORE_TILES, True)

    n_steps = (n_tiles + COUNT_TILES - 1) // COUNT_TILES

    def blank(kt, carry):
        keys_ref[pl.ds(pl.multiple_of(kt * KEY_TILE, KEY_TILE), KEY_TILE), :] = jnp.full(
            (KEY_TILE, tq), KEY_NEG_INF, I32)
        return carry

    lax.fori_loop(n_steps1 * SCORE_TILES, n_steps * COUNT_TILES, blank, 0)

    step_keys = COUNT_TILES * KEY_TILE

    def bit_step(bi, thr):
        cand = thr ^ lax.shift_left(jnp.int32(1), 31 - bi)

        def count(j, cnt):
            blk = keys_ref[pl.ds(pl.multiple_of(j * step_keys, step_keys), step_keys), :]
            ones = jnp.where(blk >= cand, 1, 0)
            return cnt + jnp.sum(ones.reshape(step_keys // SUBLANES, SUBLANES, tq), axis=0)

        cnt = lax.fori_loop(0, n_steps, count, jnp.zeros((SUBLANES, tq), I32))
        total = jnp.sum(cnt, axis=0, keepdims=True)
        return jnp.where(total >= topk, cand, thr)

    thr = lax.fori_loop(0, 32, bit_step, jnp.full((1, tq), INT_MIN, I32))
    thr = jnp.maximum(thr, KEY_NEG_INF + 1)

    def to_mask(j, carry):
        sl = pl.ds(pl.multiple_of(j * step_keys, step_keys), step_keys)
        mask = jnp.where(keys_ref[sl, :] >= thr, 0.0, MASKED)
        keys_ref[sl, :] = lax.bitcast_convert_type(mask, I32)
        return carry

    lax.fori_loop(0, n_steps, to_mask, 0)

    cols4 = GROUP * tq
    width = ATTN_TILES * KEY_TILE
    acc_ref[...] = jnp.zeros(acc_ref.shape, F32)
    m_ref[...] = jnp.full(m_ref.shape, M_INIT, F32)

    def logits(c, g):
        s0 = pl.multiple_of(c * width, width)
        q4 = q_ref[0, g * GROUP:(g + 1) * GROUP].reshape(cols4, HEAD_DIM)
        kg = k_ref[0, pl.ds(s0, width), :][:, g * HEAD_DIM:(g + 1) * HEAD_DIM]
        stage3_ref[g] = _dot_nt(kg, q4)

    def softmax(c, g, with_bias):
        s0 = pl.multiple_of(c * width, width)
        cmax = [None] * GROUP
        for r in range(width // mb):
            rows = slice(r * mb, (r + 1) * mb)
            mask = lax.bitcast_convert_type(keys_ref[pl.ds(s0 + r * mb, mb), :], F32)
            if with_bias:
                u = (r * mb) // KEY_TILE
                bias_tile = jnp.clip(c * ATTN_TILES + u - (kt_diag - 2), 0, 2)
                brows = slice((r * mb) % KEY_TILE, (r * mb) % KEY_TILE + mb)
            for hh in range(GROUP):
                cols = slice(hh * tq, (hh + 1) * tq)
                s = stage3_ref[g, rows, cols] + mask
                if with_bias:
                    s = s + bias_ref[g * GROUP + hh, bias_tile, brows, :]
                stage3_ref[g, rows, cols] = s
                part = jnp.max(s.reshape(mb // SUBLANES, SUBLANES, tq), axis=0)
                cmax[hh] = part if r == 0 else jnp.maximum(cmax[hh], part)
        cmax = [jnp.max(cm, axis=0, keepdims=True) for cm in cmax]
        m_old = m_ref[g]
        m_new = jnp.maximum(m_old, jnp.concatenate(cmax, axis=1))
        m_ref[g] = m_new
        pt_ref[g] = jnp.exp2(stage3_ref[g] - m_new).astype(BF16)
        return jnp.exp2(m_old - m_new)

    def values(c, g, alpha):
        vt = jnp.concatenate([vt_ref[0, c * ATTN_TILES + u, g] for u in range(ATTN_TILES)], axis=1)
        acc_ref[g] = acc_ref[g] * alpha + _dot(vt, pt_ref[g])

    def attend_steps(c_lo, c_hi, with_bias):
        last = jnp.maximum(c_hi - 1, 0)
        logits(jnp.minimum(c_lo, last), 0)

        def body(c, carry):
            alpha = [None] * N_KV_HEADS
            for g in range(N_KV_HEADS):
                if g + 1 < N_KV_HEADS:
                    logits(c, g + 1)
                else:
                    logits(jnp.minimum(c + 1, last), 0)
                alpha[g] = softmax(c, g, with_bias)
                if g > 0:
                    values(c, g - 1, alpha[g - 1])
            values(c, N_KV_HEADS - 1, alpha[N_KV_HEADS - 1])
            return carry

        lax.fori_loop(c_lo, c_hi, body, 0)

    n_far3 = jnp.maximum(n_steps - 2, 0)
    attend_steps(0, n_far3, False)
    attend_steps(n_far3, n_steps, True)

    for g in range(N_KV_HEADS):
        acc = acc_ref[g]
        out_t = acc[:HEAD_DIM] / acc[HEAD_DIM:HEAD_DIM + 1]
        for pair in range(GROUP // 2):
            two = jnp.concatenate([out_t[:, (2 * pair) * tq:(2 * pair + 1) * tq],
                                   out_t[:, (2 * pair + 1) * tq:(2 * pair + 2) * tq]], axis=0)
            c0 = (g * GROUP + 2 * pair) * HEAD_DIM
            o_ref[0, :, c0:c0 + 2 * HEAD_DIM] = two.T.astype(o_ref.dtype)


def _t5_bucket(rel):
    half = N_BUCKETS // 2
    max_exact = half // 2
    n = jnp.abs(rel)
    large = max_exact + (jnp.log(jnp.maximum(n, max_exact).astype(jnp.float32) / max_exact)
                         / math.log(MAX_DISTANCE / max_exact) * (half - max_exact)).astype(jnp.int32)
    large = jnp.minimum(large, half - 1)
    return jnp.where(rel > 0, half, 0) + jnp.where(n < max_exact, n, large)


def _near_bias(rel_bias):
    t = jnp.arange(KEY_TILE, dtype=I32)[None, :]
    s = jnp.arange(3 * KEY_TILE, dtype=I32)[:, None] - 2 * KEY_TILE
    bias = rel_bias[_t5_bucket(s - t)].astype(F32) - rel_bias[N_BUCKETS // 2 - 1].astype(F32)
    return (bias * LOG2E).reshape(3, KEY_TILE, KEY_TILE, N_HEADS).transpose(3, 0, 1, 2)


def _sparse_attn(qi, wt, q, ki, k, vt, bias, *, past, length, topk):
    b, _, t, _ = q.shape
    lp = k.shape[1]
    tq = KEY_TILE
    assert past % KEY_TILE == 0 and t % tq == 0 and lp % (COUNT_TILES * KEY_TILE) == 0
    assert lp >= -(-(past + t) // (COUNT_TILES * KEY_TILE)) * COUNT_TILES * KEY_TILE
    kernel = functools.partial(_sparse_attn_kernel, past=past, length=length, topk=topk)
    once = dict(pipeline_mode=pl.Buffered(1))
    return pl.pallas_call(
        kernel,
        grid=(b, t // tq),
        in_specs=[
            pl.BlockSpec((1, IDX_HEADS, tq, IDX_DIM), lambda bi, i: (bi, 0, i, 0)),
            pl.BlockSpec((1, IDX_HEADS, tq), lambda bi, i: (bi, 0, i)),
            pl.BlockSpec((1, N_HEADS, tq, HEAD_DIM), lambda bi, i: (bi, 0, i, 0)),
            pl.BlockSpec((1, lp, IDX_DIM), lambda bi, i: (bi, 0, 0), **once),
            pl.BlockSpec((1, lp, KV_WIDTH), lambda bi, i: (bi, 0, 0), **once),
            pl.BlockSpec((1, lp // KEY_TILE, N_KV_HEADS, V_ROWS, KEY_TILE), lambda bi, i: (bi, 0, 0, 0, 0), **once),
            pl.BlockSpec(bias.shape, lambda bi, i: (0, 0, 0, 0), **once),
        ],
        out_specs=pl.BlockSpec((1, tq, ATTN_WIDTH), lambda bi, i: (bi, i, 0)),
        out_shape=jax.ShapeDtypeStruct((b, t, ATTN_WIDTH), BF16),
        scratch_shapes=[
            pltpu.VMEM((lp, tq), I32),
            pltpu.VMEM((SCORE_TILES * KEY_TILE, IDX_HEADS * tq), F32),
            pltpu.VMEM((N_KV_HEADS, ATTN_TILES * KEY_TILE, GROUP * tq), F32),
            pltpu.VMEM((N_KV_HEADS, ATTN_TILES * KEY_TILE, GROUP * tq), BF16),
            pltpu.VMEM((N_KV_HEADS, V_ROWS, GROUP * tq), F32),
            pltpu.VMEM((N_KV_HEADS, 1, GROUP * tq), F32),
        ],
        compiler_params=_params(("arbitrary", "arbitrary")),
        name="sparse_attn",
    )(qi, wt, q, ki, k, vt, bias)


def _hgrn_kernel(rf_ref, rq_ref, ri_ref, rg_ref, lb_ref, gn_ref, s0_ref, tri_ref, trib_ref, ones_ref,
                 o_ref, st_ref, *, tb, chunk):
    @pl.when(pl.program_id(2) == 0)
    def _():
        st_ref[...] = s0_ref[...]

    lb = lb_ref[...]
    nsub = chunk // SUB
    row_id = lax.broadcasted_iota(I32, (SUB, REC_DK), 0)
    blk_r = lax.broadcasted_iota(I32, (chunk, chunk), 0) // SUB
    blk_c = lax.broadcasted_iota(I32, (chunk, chunk), 1) // SUB

    for c in range(tb // chunk):
        rows = slice(c * chunk, (c + 1) * chunk)
        f = lb + (1.0 - lb) * jax.nn.sigmoid(rf_ref[rows, :])
        logf = jnp.log(f)
        kk = 1.0 - f
        qq = _silu(rq_ref[rows, :])
        vv = ri_ref[rows, :]
        vv_b = vv.astype(BF16)
        b = _dot_sel_lhs(tri_ref[...], logf)
        bl = _dot_sel_lhs(trib_ref[...], logf)
        tot = [bl[(i + 1) * SUB - 1:(i + 1) * SUB, :] for i in range(nsub)]
        b_end = b[chunk - 1:chunk, :]
        st = st_ref[0, 0]

        qd = qq * jnp.exp(bl)
        kend = kk * jnp.exp(jnp.concatenate([tot[i] - bl[i * SUB:(i + 1) * SUB] for i in range(nsub)], axis=0))
        kend_b = kend.astype(BF16)

        a_off = jnp.zeros((chunk, chunk), F32)
        for dist in range(nsub - 1):
            parts = []
            for i in range(nsub):
                blk = qd[i * SUB:(i + 1) * SUB]
                if dist > 0:
                    if i - dist >= 0:
                        span = tot[i - dist]
                        for r in range(i - dist + 1, i):
                            span = span + tot[r]
                        blk = blk * jnp.exp(span)
                    else:
                        blk = jnp.zeros_like(blk)
                parts.append(blk)
            qdd = jnp.concatenate(parts, axis=0).astype(BF16)
            a_off = a_off + jnp.where(blk_r - blk_c - 1 == dist, _dot_nt(qdd, kend_b), 0.0)
        o = _dot(a_off.astype(BF16), vv_b)

        o = o + _dot_nt((qq * jnp.exp(b)).astype(BF16), st.astype(BF16))

        o_diag = []
        for i in range(nsub):
            sl = slice(i * SUB, (i + 1) * SUB)
            bli, qi_, ki_, vi = bl[sl], qq[sl], kk[sl], vv[sl]
            z = []
            for t in range(SUB):
                diff = jnp.where(row_id <= t, bli[t:t + 1, :] - bli, -jnp.inf)
                z.append(qi_[t:t + 1, :] * ki_ * jnp.exp(diff))
            zsum = _dot(jnp.concatenate(z, axis=0).astype(BF16), ones_ref[...])
            for t in range(SUB):
                o_diag.append(jnp.sum(zsum[t * SUB:(t + 1) * SUB] * vi, axis=0, keepdims=True))
        o = o + jnp.concatenate(o_diag, axis=0)

        tail = [None] * nsub
        run = jnp.zeros_like(tot[0])
        for i in range(nsub - 1, -1, -1):
            tail[i] = run
            run = run + tot[i]
        kdec = kend * jnp.exp(jnp.concatenate([jnp.broadcast_to(tail[i], (SUB, REC_DK)) for i in range(nsub)], axis=0))
        st_ref[0, 0] = st * jnp.exp(b_end) + _dot(vv.T.astype(BF16), kdec.astype(BF16))

        o_ref[rows, :] = (_rms_rows(o, gn_ref[...]) * _silu(rg_ref[rows, :])).astype(o_ref.dtype)


def _hgrn(proj, lb, gn, s0t, *, tb, chunk):
    b, t, _ = proj.shape
    nh = REC_HEADS
    tri = np.tril(np.ones((chunk, chunk), np.float32))
    blk = np.arange(chunk) // SUB
    trib = tri * (blk[:, None] == blk[None, :])
    kernel = functools.partial(_hgrn_kernel, tb=tb, chunk=chunk)
    col = lambda part: pl.BlockSpec((None, tb, REC_DK), lambda bi, h, c: (bi, c, part * nh + h))
    const = lambda a: pl.BlockSpec(a.shape, lambda bi, h, c: (0,) * a.ndim)
    tri, trib = jnp.asarray(tri, BF16), jnp.asarray(trib, BF16)
    ones = jnp.ones((REC_DK, REC_DV), BF16)
    return pl.pallas_call(
        kernel,
        grid=(b, nh, t // tb),
        in_specs=[
            col(0), col(1), col(2), col(3),
            pl.BlockSpec((1, REC_DK), lambda bi, h, c: (0, h)),
            const(gn),
            pl.BlockSpec((1, 1, REC_DV, REC_DK), lambda bi, h, c: (bi, h, 0, 0)),
            const(tri), const(trib), const(ones),
        ],
        out_specs=[
            pl.BlockSpec((None, tb, REC_DV), lambda bi, h, c: (bi, c, h)),
            pl.BlockSpec((1, 1, REC_DV, REC_DK), lambda bi, h, c: (bi, h, 0, 0)),
        ],
        out_shape=[
            jax.ShapeDtypeStruct((b, t, REC_WIDTH), BF16),
            jax.ShapeDtypeStruct((b, nh, REC_DV, REC_DK), F32),
        ],
        compiler_params=_params(("arbitrary", "arbitrary", "arbitrary")),
        name="hgrn2",
    )(proj, proj, proj, proj, lb, gn, s0t, tri, trib, ones)


def _merge_kernel(x_ref, oa_ref, ob_ref, ga_ref, gb_ref, wa_ref, wb_ref, wo_ref, o_ref, m_ref):
    @pl.when(pl.program_id(1) == 0)
    def _():
        ma = jax.nn.sigmoid(ga_ref[...]) * _dot(oa_ref[...], wa_ref[...])
        mb = jax.nn.sigmoid(gb_ref[...]) * _dot(ob_ref[...], wb_ref[...])
        m_ref[...] = (ma + mb).astype(BF16)

    o_ref[...] = x_ref[...] + _dot(m_ref[...], wo_ref[...])


def _merge(x, oa, ob, gates, wa, wb, wo, tm, tn):
    n, d = x.shape
    nd = d // tn
    once = dict(pipeline_mode=pl.Buffered(1))
    return pl.pallas_call(
        _merge_kernel,
        grid=(n // tm, nd),
        in_specs=[
            pl.BlockSpec((tm, tn), lambda i, j: (i, j)),
            pl.BlockSpec((tm, oa.shape[1]), lambda i, j: (i, 0)),
            pl.BlockSpec((tm, ob.shape[1]), lambda i, j: (i, 0)),
            pl.BlockSpec((tm, d), lambda i, j: (i, 0)),
            pl.BlockSpec((tm, d), lambda i, j: (i, 1)),
            pl.BlockSpec(wa.shape, lambda i, j: (0, 0), **once),
            pl.BlockSpec(wb.shape, lambda i, j: (0, 0), **once),
            pl.BlockSpec((d, tn), lambda i, j: (0, j)),
        ],
        out_specs=pl.BlockSpec((tm, tn), lambda i, j: (i, j)),
        out_shape=jax.ShapeDtypeStruct((n, d), F32),
        scratch_shapes=[pltpu.VMEM((tm, d), BF16)],
        compiler_params=_params(("arbitrary", "arbitrary")),
        name="merge",
    )(x, oa, ob, gates, gates, wa, wb, wo)


def _pick(n, pref):
    return pref if n % pref == 0 else n


def _layer(x, past_k, past_v, past_ki, s0, lw):
    bsz, t, d = x.shape
    n = bsz * t
    past = past_k.shape[1]
    length = past + t
    topk = min(TOPK_MAX, length // 4)
    tm = _pick(n, 512)
    x2 = x.reshape(n, d)

    x1 = _ffn(x2, lw["norm_ffa"], lw["w_ffa_in"], lw["w_ffa_out"], tm, 512)

    q, k, kb, v, vb, qi, ki, kib, wi = _attn_prep(
        x1, lw["norm_mix"], lw["w_attn_in"], lw["gain_q"], lw["gain_k"], tm)
    proj_rec = _norm_proj(x1, lw["norm_mix"], lw["w_rec_in"], tm, 2048, "proj_rec")
    gates = _norm_proj(x1, lw["norm_mix"], lw["w_gate_in"], tm, _pick(2 * d, 2048), "proj_gate")

    tpad = -(-t // KEY_TILE) * KEY_TILE
    step = COUNT_TILES * KEY_TILE
    lp = -(-(past + tpad) // step) * step

    def keys(past_x, new_x, width):
        full = jnp.concatenate([past_x.reshape(bsz, past, width).astype(BF16), new_x.reshape(bsz, t, width)], axis=1)
        return jnp.pad(full, ((0, 0), (0, lp - length), (0, 0)))

    def heads(a):
        a = jnp.pad(a.reshape(bsz, t, N_HEADS, HEAD_DIM), ((0, 0), (0, tpad - t), (0, 0), (0, 0)))
        return a.transpose(0, 2, 1, 3)

    wt = jnp.pad(wi.reshape(bsz, t, IDX_HEADS), ((0, 0), (0, tpad - t), (0, 0))).transpose(0, 2, 1)
    v_all = keys(past_v, vb, KV_WIDTH).reshape(bsz, lp // KEY_TILE, KEY_TILE, N_KV_HEADS, HEAD_DIM)
    vt = jnp.concatenate(
        [v_all.transpose(0, 1, 3, 4, 2),
         jnp.ones((bsz, lp // KEY_TILE, N_KV_HEADS, V_ROWS - HEAD_DIM, KEY_TILE), BF16)], axis=3)
    o_attn = _sparse_attn(
        heads(qi), wt, heads(q), keys(past_ki, kib, IDX_DIM), keys(past_k, kb, KV_WIDTH), vt,
        _near_bias(lw["rel_bias"]), past=past, length=length, topk=topk)
    o_attn = o_attn[:, :t].reshape(n, ATTN_WIDTH)

    chunk = min(CHUNK, t)
    o_rec, st = _hgrn(proj_rec.reshape(bsz, t, 4 * REC_WIDTH), lw["lower_bound"], lw["hgrn_norm"],
                      s0.swapaxes(-1, -2), tb=_pick(t, 512), chunk=chunk)

    x2 = _merge(x1, o_attn, o_rec.reshape(n, REC_WIDTH), gates,
                lw["w_branch_attn"], lw["w_branch_rec"], lw["w_out"], tm, _pick(d, 512))
    y = _ffn(x2, lw["norm_ffb"], lw["w_ffb_in"], lw["w_ffb_out"], tm, 512)
    return (y.reshape(bsz, t, d), k.reshape(bsz, t, N_KV_HEADS, HEAD_DIM),
            v.reshape(bsz, t, N_KV_HEADS, HEAD_DIM), ki, st.swapaxes(-1, -2).astype(s0.dtype))


def _layer_weights(l, lower_bounds, norm_ffa, w_ffa_in, w_ffa_out, norm_mix, w_in, qk_gain_q, qk_gain_k,
                   rel_bias, hgrn_norm, w_branch_attn, w_branch_rec, w_out, norm_ffb, w_ffb_in, w_ffb_out):
    w = w_in[l]
    widths = (ATTN_WIDTH, KV_WIDTH, KV_WIDTH, IDX_HEADS * IDX_DIM, IDX_DIM, IDX_HEADS,
              REC_WIDTH, REC_WIDTH, REC_WIDTH, REC_WIDTH)
    splits = np.cumsum(widths)
    d = w.shape[0]
    pad = lambda a: jnp.pad(a, ((0, 0), (0, LANES - a.shape[1])))
    w_attn_in = jnp.concatenate(
        [w[:, :splits[3]], pad(w[:, splits[3]:splits[4]]), pad(w[:, splits[4]:splits[5]])], axis=1)
    row = lambda a: a.reshape(1, -1).astype(F32)
    return dict(
        norm_ffa=row(norm_ffa[l]), w_ffa_in=w_ffa_in[l].astype(BF16), w_ffa_out=w_ffa_out[l].astype(BF16),
        norm_mix=row(norm_mix[l]),
        w_attn_in=w_attn_in.astype(BF16),
        w_rec_in=w[:, splits[5]:splits[9]].astype(BF16),
        w_gate_in=w[:, splits[9]:splits[9] + 2 * d].astype(BF16),
        gain_q=row(jnp.tile(qk_gain_q[l], N_HEADS) * (HEAD_DIM ** -0.5 * LOG2E)),
        gain_k=row(jnp.tile(qk_gain_k[l], N_KV_HEADS)),
        rel_bias=rel_bias,
        lower_bound=row(lower_bounds[l]),
        hgrn_norm=row(hgrn_norm[l]),
        w_branch_attn=w_branch_attn[l].astype(BF16), w_branch_rec=w_branch_rec[l].astype(BF16),
        w_out=w_out[l].astype(BF16),
        norm_ffb=row(norm_ffb[l]), w_ffb_in=w_ffb_in[l].astype(BF16), w_ffb_out=w_ffb_out[l].astype(BF16),
    )


def _trunk(x, past_k, past_v, past_ki, s0, layers):
    ks, vs, kis, ss = [], [], [], []
    for l, lw in enumerate(layers):
        x, k, v, ki, s = _layer(x, past_k[l], past_v[l], past_ki[l], s0[l], lw)
        ks.append(k)
        vs.append(v)
        kis.append(ki.reshape(k.shape[0], k.shape[1], IDX_DIM))
        ss.append(s)
    return x, jnp.stack(ks), jnp.stack(vs), jnp.stack(kis), jnp.stack(ss)


def kernel(x_prompt, x_sample, cache_k, cache_v, cache_kidx, state_hgrn, norm_ffa, w_ffa_in, w_ffa_out,
           norm_mix, w_in, qk_gain_q, qk_gain_k, rel_bias, hgrn_lb_raw, hgrn_norm, w_branch_attn,
           w_branch_rec, w_out, norm_ffb, w_ffb_in, w_ffb_out):
    depth = w_in.shape[0]
    lb_p = jax.nn.softmax(hgrn_lb_raw.astype(F32), axis=0)
    lower_bounds = jnp.cumsum(lb_p, axis=0)[:depth]
    layers = [_layer_weights(l, lower_bounds, norm_ffa, w_ffa_in, w_ffa_out, norm_mix, w_in, qk_gain_q,
                             qk_gain_k, rel_bias, hgrn_norm, w_branch_attn, w_branch_rec, w_out,
                             norm_ffb, w_ffb_in, w_ffb_out) for l in range(depth)]
    bp = x_prompt.shape[0]
    dt = x_prompt.dtype
    empty_k = jnp.zeros((depth, bp, 0, N_KV_HEADS, HEAD_DIM), dt)
    empty_ki = jnp.zeros((depth, bp, 0, IDX_DIM), dt)
    zero_s = jnp.zeros((depth, bp, REC_HEADS, REC_DK, REC_DV), state_hgrn.dtype)
    y_p, k_p, v_p, ki_p, s_p = _trunk(x_prompt, empty_k, empty_k, empty_ki, zero_s, layers)
    y_s, k_s, v_s, ki_s, s_s = _trunk(x_sample, cache_k, cache_v, cache_kidx, state_hgrn, layers)
    return (y_p, y_s, k_p, v_p, ki_p, s_p, k_s, v_s, ki_s, s_s)
```
